```python
import functools
import jax, jax.numpy as jnp
from jax import lax
import numpy as np

D_MODEL = 1024
BATCH = 16
SEQ = 2048
DEPTH = 4

GRID_W = 64
CTX_LEN = 256
N_MIXERS = 2
N_A = (DEPTH + 1) // 2
N_B = DEPTH // 2
N_MOD = 9
D_FF = 2816
CONV_DIM = D_MODEL
CONV_W = 3
MLA_HEADS = 8
QK_NOPE = 128
QK_ROPE = 64
QK_HEAD = QK_NOPE + QK_ROPE
V_HEAD = 128
Q_LORA = 256
KV_LORA = 128
ROPE_BASE = 10000.0
QK_SCALE = QK_HEAD ** -0.5
Q_BLOCK = 128
EPS = 1e-6

kernel_name = "hybrid_shortconv_mla_macaron_dit"


def rms_norm(x, g):
    xf = x.astype(jnp.float32)
    y = xf * lax.rsqrt(jnp.mean(xf * xf, axis=-1, keepdims=True) + EPS)
    return (y * g.astype(jnp.float32)).astype(x.dtype)


def adaln_chunks(cond, w_mod, b_mod):
    m = jax.nn.silu(cond) @ w_mod + b_mod
    return jnp.split(m[:, None, :], N_MOD, axis=-1)


def pre(h, g, shift, scale):
    return rms_norm(h, g) * (1 + scale) + shift


def swiglu(h, w1, w3, w2):
    return (jax.nn.silu(h @ w1) * (h @ w3)) @ w2


def conv3_centred(u, w):
    return lax.conv_general_dilated(
        u, w[:, None, :].astype(u.dtype), window_strides=(1,), padding=((1, 1),),
        dimension_numbers=("NWC", "WIO", "NWC"), feature_group_count=u.shape[-1])


def short_conv_mixer(h, w_in, conv_w, w_out):
    b_gate, c_gate, u = jnp.split(h @ w_in, 3, axis=-1)
    return (b_gate * conv3_centred(c_gate * u, conv_w)) @ w_out


def axial_rope_tables(n):
    rows = n // GRID_W
    r = jnp.broadcast_to(jnp.arange(rows)[:, None], (rows, GRID_W)).reshape(n).astype(jnp.float32)
    col = jnp.broadcast_to(jnp.arange(GRID_W)[None, :], (rows, GRID_W)).reshape(n).astype(jnp.float32)
    n_freq = QK_ROPE // 4
    inv = ROPE_BASE ** (-jnp.arange(n_freq, dtype=jnp.float32) / n_freq)
    ang = jnp.stack([r[:, None] * inv, col[:, None] * inv], axis=1)
    return jnp.cos(ang), jnp.sin(ang)


def apply_axial_rope(t, cos, sin):
    ts = t.reshape(t.shape[:-1] + (2, 2, QK_ROPE // 4))
    x1, x2 = ts[..., 0, :], ts[..., 1, :]
    cos = cos.astype(t.dtype)
    sin = sin.astype(t.dtype)
    y = jnp.stack([x1 * cos - x2 * sin, x1 * sin + x2 * cos], axis=-2)
    return y.reshape(t.shape)


def rope_tail(t, cos, sin):
    return jnp.concatenate([t[..., :QK_NOPE], apply_axial_rope(t[..., QK_NOPE:], cos, sin)], axis=-1)


def mla_down(h, w_a):
    return jnp.split(h @ w_a, [Q_LORA, Q_LORA + KV_LORA], axis=-1)


def mla_queries(cq, g_qa, w_uq, g_q):
    b, n, _ = cq.shape
    q = (rms_norm(cq, g_qa) @ w_uq).reshape(b, n, MLA_HEADS, QK_HEAD)
    return rms_norm(q, g_q).transpose(0, 2, 1, 3)


def mla_keys_values(ckv, k_rope, g_kva, w_ukv, g_k):
    b, n, _ = ckv.shape
    kv = (rms_norm(ckv, g_kva) @ w_ukv).reshape(b, n, MLA_HEADS, QK_NOPE + V_HEAD)
    k_nope, v = jnp.split(kv, [QK_NOPE], axis=-1)
    k_r = jnp.broadcast_to(k_rope[:, :, None, :], (b, n, MLA_HEADS, QK_ROPE))
    k = rms_norm(jnp.concatenate([k_nope, k_r], axis=-1), g_k)
    return k.transpose(0, 2, 1, 3), v.transpose(0, 2, 1, 3)


def softmax_attend(q, k, v):
    s = jnp.einsum("bhqd,bhkd->bhqk", q, k).astype(jnp.float32) * QK_SCALE
    p = jax.nn.softmax(s, axis=-1).astype(v.dtype)
    return jnp.einsum("bhqk,bhkd->bhqd", p, v)


def merge_heads(o):
    b, h, n, d = o.shape
    return o.transpose(0, 2, 1, 3).reshape(b, n, h * d)


def latent_attention(q, k_all, v_all):
    b, h, n, dq = q.shape
    nb = n // Q_BLOCK
    qb = q.reshape(b, h, nb, Q_BLOCK, dq).transpose(2, 0, 1, 3, 4)
    o = lax.map(lambda qblk: softmax_attend(qblk, k_all, v_all), qb)
    return o.transpose(1, 0, 3, 2, 4).reshape(b, n, h * V_HEAD)


def setup_inputs(seed: int = 0) -> dict:
    key = jax.random.key(seed)
    ks = jax.random.split(key, 24)
    f32 = jnp.float32

    def nrm(k, shape, scale):
        return jax.random.normal(k, shape, f32) * scale

    def gain(k, shape):
        return 1.0 + 0.1 * jax.random.normal(k, shape, f32)

    D = D_MODEL
    return {
        "x": nrm(ks[0], (BATCH, SEQ, D), 1.0),
        "c": nrm(ks[1], (BATCH, D), 1.0),
        "ctx": nrm(ks[2], (BATCH, CTX_LEN, D), 1.0),
        "c_ctx": nrm(ks[3], (D,), 1.0),
        "w_mod": nrm(ks[4], (DEPTH, D, N_MOD * D), 0.5 * D ** -0.5),
        "b_mod": nrm(ks[5], (DEPTH, N_MOD * D), 0.02),
        "g_norm": gain(ks[6], (DEPTH, 3, D)),
        "ffn_w1": nrm(ks[7], (DEPTH, 2, D, D_FF), D ** -0.5),
        "ffn_w3": nrm(ks[8], (DEPTH, 2, D, D_FF), D ** -0.5),
        "ffn_w2": nrm(ks[9], (DEPTH, 2, D_FF, D), D_FF ** -0.5),
        "sc_w_in": nrm(ks[10], (N_A, D, 3 * CONV_DIM), D ** -0.5),
        "sc_conv": nrm(ks[11], (N_A, CONV_W, CONV_DIM), CONV_W ** -0.5),
        "sc_w_out": nrm(ks[12], (N_A, CONV_DIM, D), CONV_DIM ** -0.5),
        "mla_w_a": nrm(ks[13], (N_B, D, Q_LORA + KV_LORA + QK_ROPE), D ** -0.5),
        "mla_g_qa": gain(ks[14], (N_B, Q_LORA)),
        "mla_w_uq": nrm(ks[15], (N_B, Q_LORA, MLA_HEADS * QK_HEAD), Q_LORA ** -0.5),
        "mla_g_kva": gain(ks[16], (N_B, KV_LORA)),
        "mla_w_ukv": nrm(ks[17], (N_B, KV_LORA, MLA_HEADS * (QK_NOPE + V_HEAD)), KV_LORA ** -0.5),
        "mla_g_q": gain(ks[18], (N_B, QK_HEAD)),
        "mla_g_k": gain(ks[19], (N_B, QK_HEAD)),
        "mla_w_o": nrm(ks[20], (N_B, MLA_HEADS * V_HEAD, D), (MLA_HEADS * V_HEAD) ** -0.5),
    }


def reference(x, c, ctx, c_ctx, w_mod, b_mod, g_norm, ffn_w1, ffn_w3, ffn_w2,
              sc_w_in, sc_conv, sc_w_out, mla_w_a, mla_g_qa, mla_w_uq, mla_g_kva,
              mla_w_ukv, mla_g_q, mla_g_k, mla_w_o):
    n = x.shape[1]
    cos, sin = axial_rope_tables(n)
    h_x, h_c = x, ctx
    for i in range(DEPTH):
        kind, j = i % N_MIXERS, i // N_MIXERS
        last = i == DEPTH - 1
        run_ctx_in = (not last) or kind == 1
        run_ctx_out = not last

        mx = adaln_chunks(c, w_mod[i], b_mod[i])
        mc = adaln_chunks(c_ctx[None], w_mod[i], b_mod[i])
        ffn1 = functools.partial(swiglu, w1=ffn_w1[i, 0], w3=ffn_w3[i, 0], w2=ffn_w2[i, 0])
        ffn2 = functools.partial(swiglu, w1=ffn_w1[i, 1], w3=ffn_w3[i, 1], w2=ffn_w2[i, 1])

        h_x = h_x + 0.5 * mx[2] * ffn1(pre(h_x, g_norm[i, 0], mx[0], mx[1]))
        if run_ctx_in:
            h_c = h_c + 0.5 * mc[2] * ffn1(pre(h_c, g_norm[i, 0], mc[0], mc[1]))

        nx = pre(h_x, g_norm[i, 1], mx[3], mx[4])
        if kind == 0:
            ox = short_conv_mixer(nx, sc_w_in[j], sc_conv[j], sc_w_out[j])
            if run_ctx_out:
                nc = pre(h_c, g_norm[i, 1], mc[3], mc[4])
                oc = short_conv_mixer(nc, sc_w_in[j], sc_conv[j], sc_w_out[j])
        else:
            nc = pre(h_c, g_norm[i, 1], mc[3], mc[4])
            cq_c, ckv_c, kr_c = mla_down(nc, mla_w_a[j])
            k_c, v_c = mla_keys_values(ckv_c, kr_c, mla_g_kva[j], mla_w_ukv[j], mla_g_k[j])
            cq_x, ckv_x, kr_x = mla_down(nx, mla_w_a[j])
            k_x, v_x = mla_keys_values(ckv_x, kr_x, mla_g_kva[j], mla_w_ukv[j], mla_g_k[j])
            k_x = rope_tail(k_x, cos, sin)
            q_x = rope_tail(mla_queries(cq_x, mla_g_qa[j], mla_w_uq[j], mla_g_q[j]), cos, sin)
            k_all = jnp.concatenate([k_c, k_x], axis=2)
            v_all = jnp.concatenate([v_c, v_x], axis=2)
            ox = latent_attention(q_x, k_all, v_all) @ mla_w_o[j]
            if run_ctx_out:
                q_c = mla_queries(cq_c, mla_g_qa[j], mla_w_uq[j], mla_g_q[j])
                oc = merge_heads(softmax_attend(q_c, k_c, v_c)) @ mla_w_o[j]
        h_x = h_x + mx[5] * ox
        if run_ctx_out:
            h_c = h_c + mc[5] * oc

        h_x = h_x + 0.5 * mx[8] * ffn2(pre(h_x, g_norm[i, 2], mx[6], mx[7]))
        if run_ctx_out:
            h_c = h_c + 0.5 * mc[8] * ffn2(pre(h_c, g_norm[i, 2], mc[6], mc[7]))
    return h_x
```

```python
import functools

import jax
import jax.numpy as jnp
import numpy as np
from jax.experimental import pallas as pl
from jax.experimental.pallas import tpu as pltpu

GRID_W = 64
N_MOD = 9
MLA_HEADS = 8
QK_NOPE = 128
QK_ROPE = 64
QK_HEAD = QK_NOPE + QK_ROPE
V_HEAD = 128
Q_LORA = 256
KV_LORA = 128
ROPE_BASE = 10000.0
QK_SCALE = QK_HEAD ** -0.5
EPS = 1e-6

LANES = 128
QK_PAD = 2 * LANES
VMEM_LIMIT = 56 * 1024 * 1024
MOD_ROWS = 24

BF16 = jnp.bfloat16
F32 = jnp.float32


def _dot(a, b):
    return jnp.dot(a, b, preferred_element_type=F32)


def _resident(shape):
    nd = len(shape)
    return pl.BlockSpec(shape, lambda *_: (0,) * nd, pipeline_mode=pl.Buffered(1))


def _params(n_grid):
    return pltpu.CompilerParams(
        dimension_semantics=("arbitrary",) * n_grid, vmem_limit_bytes=VMEM_LIMIT)


def _pre(h, g, shift, scale):
    ms = jnp.mean(h * h, axis=-1, keepdims=True)
    return (h * jax.lax.rsqrt(ms + EPS) * g) * (1.0 + scale) + shift


def _silu(a):
    return a * jax.nn.sigmoid(a)


def _mod_kernel(cond_ref, w_ref, b_ref, o_ref):
    cond = _silu(cond_ref[...]).astype(BF16)
    o_ref[0] = _dot(cond, w_ref[0].astype(BF16)) + b_ref[0]


def _modulation(cond, w_mod, b_mod):
    depth, d, nm = w_mod.shape
    tn = nm // 4
    return pl.pallas_call(
        _mod_kernel,
        grid=(depth, nm // tn),
        in_specs=[
            pl.BlockSpec((MOD_ROWS, d), lambda i, j: (0, 0)),
            pl.BlockSpec((1, d, tn), lambda i, j: (i, 0, j)),
            pl.BlockSpec((1, 1, tn), lambda i, j: (i, 0, j)),
        ],
        out_specs=pl.BlockSpec((1, MOD_ROWS, tn), lambda i, j: (i, 0, j)),
        out_shape=jax.ShapeDtypeStruct((depth, MOD_ROWS, nm), F32),
        compiler_params=_params(2),
        name="adaln_mod",
    )(cond, w_mod, b_mod.reshape(depth, 1, nm))


def _mod_spec(layer, sub, per_sample, d):
    if per_sample:
        return pl.BlockSpec((1, 1, 1, 3, d), lambda b, t: (layer, b, sub, 0, 0))
    ctx_row = MOD_ROWS - 8
    return pl.BlockSpec((1, 1, 1, 3, d), lambda b, t: (layer, ctx_row, sub, 0, 0))


def _ffn_kernel(h_ref, mod_ref, g_ref, w1_ref, w3_ref, w2_ref, o_ref):
    h = h_ref[0]
    mod = mod_ref[0, 0, 0]
    xn = _pre(h, g_ref[...], mod[0:1], mod[1:2]).astype(BF16)
    a = _dot(xn, w1_ref[...])
    b = _dot(xn, w3_ref[...])
    gated = (_silu(a) * b).astype(BF16)
    f = _dot(gated, w2_ref[...])
    o_ref[0] = h + (0.5 * mod[2:3]) * f


def _ffn(h, mod, layer, sub, per_sample, g, w1, w3, w2, tm):
    bsz, n, d = h.shape
    dff = w1.shape[1]
    tile = pl.BlockSpec((1, tm, d), lambda b, t: (b, t, 0))
    return pl.pallas_call(
        _ffn_kernel,
        grid=(bsz, n // tm),
        in_specs=[tile, _mod_spec(layer, sub, per_sample, d), _resident((1, d)),
                  _resident((d, dff)), _resident((d, dff)), _resident((dff, d))],
        out_specs=tile,
        out_shape=jax.ShapeDtypeStruct(h.shape, F32),
        compiler_params=_params(2),
        name="ffn",
    )(h, mod, g, w1, w3, w2)


def _shortconv_kernel(h_ref, hp_ref, hn_ref, mod_ref, g_ref, win_ref, cw_ref, wout_ref, o_ref):
    t = pl.program_id(1)
    nt = pl.num_programs(1)
    h = h_ref[0]
    tm, d = h.shape
    mod = mod_ref[0, 0, 0]
    g = g_ref[...]
    xn = _pre(h, g, mod[0:1], mod[1:2]).astype(BF16)
    p = _dot(xn, win_ref[...])
    b_gate = p[:, :d]
    cu = p[:, d:2 * d] * p[:, 2 * d:]

    halo = jnp.concatenate([hp_ref[0], hn_ref[0]], axis=0)
    hx = _pre(halo, g, mod[0:1], mod[1:2]).astype(BF16)
    ph = _dot(hx, win_ref[:, d:])
    cu_halo = ph[:, :d] * ph[:, d:]
    prev_row = jnp.where(t > 0, cu_halo[7:8], 0.0)
    next_row = jnp.where(t < nt - 1, cu_halo[8:9], 0.0)

    row = jax.lax.broadcasted_iota(jnp.int32, (tm, d), 0)
    cu_prev = jnp.where(row == 0, prev_row, pltpu.roll(cu, 1, 0))
    cu_next = jnp.where(row == tm - 1, next_row, pltpu.roll(cu, tm - 1, 0))
    cw = cw_ref[...]
    conv = cw[0:1] * cu_prev + cw[1:2] * cu + cw[2:3] * cu_next
    y = (b_gate * conv).astype(BF16)
    o_ref[0] = h + mod[2:3] * _dot(y, wout_ref[...])


def _shortconv(h, mod, layer, per_sample, g, w_in, conv_w, w_out, tm):
    bsz, n, d = h.shape
    tile = pl.BlockSpec((1, tm, d), lambda b, t: (b, t, 0))
    r = tm // 8
    last = n // 8 - 1
    prev = pl.BlockSpec((1, 8, d), lambda b, t: (b, jnp.maximum(t * r - 1, 0), 0))
    nxt = pl.BlockSpec((1, 8, d), lambda b, t: (b, jnp.minimum((t + 1) * r, last), 0))
    return pl.pallas_call(
        _shortconv_kernel,
        grid=(bsz, n // tm),
        in_specs=[tile, prev, nxt, _mod_spec(layer, 1, per_sample, d), _resident((1, d)),
                  _resident(w_in.shape), _resident(conv_w.shape), _resident(w_out.shape)],
        out_specs=tile,
        out_shape=jax.ShapeDtypeStruct(h.shape, F32),
        compiler_params=_params(2),
        name="shortconv",
    )(h, h, h, mod, g, w_in, conv_w, w_out)


def _mla_proj_kernel(h_ref, mod_ref, g_ref, wa_ref, gqa_ref, wuq_ref, gkva_ref, wukv_ref,
                     gq_ref, gk_ref, cos_ref, sin_ref, q_ref, k_ref, v_ref):
    h = h_ref[0]
    mod = mod_ref[0, 0, 0]
    xn = _pre(h, g_ref[...], mod[0:1], mod[1:2]).astype(BF16)
    a = _dot(xn, wa_ref[...])
    cq = a[:, :Q_LORA]
    ckv = a[:, Q_LORA:Q_LORA + KV_LORA]
    kr = a[:, Q_LORA + KV_LORA:Q_LORA + KV_LORA + LANES]
    krs = a[:, Q_LORA + KV_LORA + LANES:]

    def rms(x, gain):
        return x * jax.lax.rsqrt(jnp.mean(x * x, axis=-1, keepdims=True) + EPS) * gain

    qraw = _dot(rms(cq, gqa_ref[...]).astype(BF16), wuq_ref[...])
    kvraw = _dot(rms(ckv, gkva_ref[...]).astype(BF16), wukv_ref[...])

    cos = cos_ref[...]
    sin = sin_ref[...]
    gq = gq_ref[...]
    gk = gk_ref[...]
    hw = MLA_HEADS * LANES
    kr_rot = kr * (cos * gk[1:2]) + krs * (sin * gk[2:3])
    kr_ss = jnp.sum(kr * kr, axis=-1, keepdims=True)
    q_cos = cos * gq[1:2]
    q_sin = sin * gq[2:3]
    for hd in range(MLA_HEADS):
        lo = hd * LANES
        qn = qraw[:, lo:lo + LANES]
        qr = qraw[:, hw + lo:hw + lo + LANES]
        qrs = qraw[:, 2 * hw + lo:2 * hw + lo + LANES]
        ss = jnp.sum(qn * qn, axis=-1, keepdims=True) + jnp.sum(qr * qr, axis=-1, keepdims=True)
        r = jax.lax.rsqrt(ss * (1.0 / QK_HEAD) + EPS) * QK_SCALE
        q_ref[0, hd, :, :LANES] = (qn * gq[0:1] * r).astype(BF16)
        q_ref[0, hd, :, LANES:] = ((qr * q_cos + qrs * q_sin) * r).astype(BF16)

        kn = kvraw[:, lo:lo + LANES]
        ssk = jnp.sum(kn * kn, axis=-1, keepdims=True) + kr_ss
        rk = jax.lax.rsqrt(ssk * (1.0 / QK_HEAD) + EPS)
        k_ref[0, hd, :, :LANES] = (kn * gk[0:1] * rk).astype(BF16)
        k_ref[0, hd, :, LANES:] = (kr_rot * rk).astype(BF16)
        v_ref[0, hd] = kvraw[:, hw + lo:hw + lo + LANES].astype(BF16)


def _mla_proj(h, mod, layer, per_sample, g, wts, cos, sin, tm):
    bsz, n, d = h.shape
    tile = pl.BlockSpec((1, tm, d), lambda b, t: (b, t, 0))
    tab = pl.BlockSpec((tm, LANES), lambda b, t: (t, 0))
    qk_spec = pl.BlockSpec((1, MLA_HEADS, tm, QK_PAD), lambda b, t: (b, 0, t, 0))
    v_spec = pl.BlockSpec((1, MLA_HEADS, tm, V_HEAD), lambda b, t: (b, 0, t, 0))
    qk_shape = jax.ShapeDtypeStruct((bsz, MLA_HEADS, n, QK_PAD), BF16)
    v_shape = jax.ShapeDtypeStruct((bsz, MLA_HEADS, n, V_HEAD), BF16)
    small = [wts["w_a"], wts["g_qa"], wts["w_uq"], wts["g_kva"], wts["w_ukv"], wts["g_q"], wts["g_k"]]
    return pl.pallas_call(
        _mla_proj_kernel,
        grid=(bsz, n // tm),
        in_specs=[tile, _mod_spec(layer, 1, per_sample, d), _resident((1, d))]
                 + [_resident(w.shape) for w in small] + [tab, tab],
        out_specs=[qk_spec, qk_spec, v_spec],
        out_shape=[qk_shape, qk_shape, v_shape],
        compiler_params=_params(2),
        name="mla_proj",
    )(h, mod, g, *small, cos, sin)


def _attn_kernel(*refs, n_parts):
    q_ref = refs[0]
    k_refs = refs[1:1 + n_parts]
    v_refs = refs[1 + n_parts:1 + 2 * n_parts]
    o_ref = refs[1 + 2 * n_parts]
    q = q_ref[0, 0]
    contract_last = (((1,), (1,)), ((), ()))
    s = [jax.lax.dot_general(q, k[0, 0], contract_last, preferred_element_type=F32) for k in k_refs]
    m = functools.reduce(jnp.maximum, [jnp.max(x, axis=-1, keepdims=True) for x in s])
    p = [jnp.exp(x - m) for x in s]
    denom = functools.reduce(jnp.add, [jnp.sum(x, axis=-1, keepdims=True) for x in p])
    o = functools.reduce(jnp.add, [_dot(x.astype(BF16), v[0, 0]) for x, v in zip(p, v_refs)])
    o_ref[0] = (o / denom).astype(o_ref.dtype)


def _attention(q, ks, vs, tq):
    bsz, nh, n, _ = q.shape
    n_parts = len(ks)
    q_spec = pl.BlockSpec((1, 1, tq, QK_PAD), lambda b, h, t: (b, h, t, 0))
    k_specs = [pl.BlockSpec((1, 1, k.shape[2], QK_PAD), lambda b, h, t: (b, h, 0, 0)) for k in ks]
    v_specs = [pl.BlockSpec((1, 1, v.shape[2], V_HEAD), lambda b, h, t: (b, h, 0, 0)) for v in vs]
    return pl.pallas_call(
        functools.partial(_attn_kernel, n_parts=n_parts),
        grid=(bsz, nh, n // tq),
        in_specs=[q_spec] + k_specs + v_specs,
        out_specs=pl.BlockSpec((1, tq, V_HEAD), lambda b, h, t: (b, t, h)),
        out_shape=jax.ShapeDtypeStruct((bsz, n, nh * V_HEAD), BF16),
        compiler_params=_params(3),
        name="attention",
    )(q, *ks, *vs)


def _outproj_kernel(h_ref, o_ref, mod_ref, wo_ref, out_ref):
    mod = mod_ref[0, 0, 0]
    out_ref[0] = h_ref[0] + mod[2:3] * _dot(o_ref[0], wo_ref[...])


def _outproj(h, o, mod, layer, per_sample, w_o, tm):
    bsz, n, d = h.shape
    tile = pl.BlockSpec((1, tm, d), lambda b, t: (b, t, 0))
    o_tile = pl.BlockSpec((1, tm, o.shape[2]), lambda b, t: (b, t, 0))
    return pl.pallas_call(
        _outproj_kernel,
        grid=(bsz, n // tm),
        in_specs=[tile, o_tile, _mod_spec(layer, 1, per_sample, d), _resident(w_o.shape)],
        out_specs=tile,
        out_shape=jax.ShapeDtypeStruct(h.shape, F32),
        compiler_params=_params(2),
        name="attn_outproj",
    )(h, o, mod, w_o)


_SWAP = np.concatenate([np.arange(16, 32), np.arange(0, 16), np.arange(48, 64), np.arange(32, 48)])


def _pad_lanes(w):
    return jnp.concatenate([w, jnp.zeros_like(w)], axis=-1)


def _mla_weights(w_a, g_qa, w_uq, g_kva, w_ukv, g_q, g_k):
    d = w_a.shape[0]
    kr = w_a[:, Q_LORA + KV_LORA:]
    w_a2 = jnp.concatenate(
        [w_a[:, :Q_LORA + KV_LORA], _pad_lanes(kr), _pad_lanes(kr[:, _SWAP])], axis=1)
    uq = w_uq.reshape(Q_LORA, MLA_HEADS, QK_HEAD)
    uq_rope = uq[:, :, QK_NOPE:]
    w_uq2 = jnp.concatenate(
        [uq[:, :, :QK_NOPE].reshape(Q_LORA, -1),
         _pad_lanes(uq_rope).reshape(Q_LORA, -1),
         _pad_lanes(uq_rope[:, :, _SWAP]).reshape(Q_LORA, -1)], axis=1)
    ukv = w_ukv.reshape(KV_LORA, MLA_HEADS, QK_NOPE + V_HEAD)
    w_ukv2 = jnp.concatenate(
        [ukv[:, :, :QK_NOPE].reshape(KV_LORA, -1), ukv[:, :, QK_NOPE:].reshape(KV_LORA, -1)], axis=1)

    def gains(gv):
        rope = gv[QK_NOPE:]
        return jnp.stack([gv[:QK_NOPE], _pad_lanes(rope), _pad_lanes(rope[_SWAP])])

    del d
    return {
        "w_a": w_a2.astype(BF16), "g_qa": g_qa[None], "w_uq": w_uq2.astype(BF16),
        "g_kva": g_kva[None], "w_ukv": w_ukv2.astype(BF16), "g_q": gains(g_q), "g_k": gains(g_k),
    }


def _rope_tables(n):
    pos = np.arange(n)
    n_freq = QK_ROPE // 4
    inv = ROPE_BASE ** (-np.arange(n_freq, dtype=np.float64) / n_freq)
    ang_r = (pos // GRID_W).astype(np.float64)[:, None] * inv
    ang_c = (pos % GRID_W).astype(np.float64)[:, None] * inv
    zeros = np.zeros((n, LANES - QK_ROPE), np.float64)
    cos = np.concatenate([np.cos(ang_r), np.cos(ang_r), np.cos(ang_c), np.cos(ang_c), zeros], axis=1)
    sin = np.concatenate([-np.sin(ang_r), np.sin(ang_r), -np.sin(ang_c), np.sin(ang_c), zeros], axis=1)
    return jnp.asarray(cos, F32), jnp.asarray(sin, F32)


def _identity_tables(n):
    ones = np.concatenate([np.ones((n, QK_ROPE), np.float32), np.zeros((n, LANES - QK_ROPE), np.float32)], axis=1)
    return jnp.asarray(ones, F32), jnp.zeros((n, LANES), F32)


def kernel(x, c, ctx, c_ctx, w_mod, b_mod, g_norm, ffn_w1, ffn_w3, ffn_w2, sc_w_in, sc_conv, sc_w_out,
           mla_w_a, mla_g_qa, mla_w_uq, mla_g_kva, mla_w_ukv, mla_g_q, mla_g_k, mla_w_o):
    bsz, n, d = x.shape
    n_ctx = ctx.shape[1]
    depth = w_mod.shape[0]
    tm_x = 512
    tm_c = n_ctx

    cond = jnp.concatenate([c, c_ctx[None], jnp.zeros((MOD_ROWS - bsz - 1, d), F32)], axis=0)
    mod = _modulation(cond, w_mod, b_mod).reshape(depth, MOD_ROWS, 3, 3, d)

    cos_x, sin_x = _rope_tables(n)
    cos_c, sin_c = _identity_tables(n_ctx)

    h_x, h_c = x, ctx
    for i in range(depth):
        kind, j = i % 2, i // 2
        last = i == depth - 1
        run_ctx_in = (not last) or kind == 1
        run_ctx_out = not last

        def ffn(h, sub, per_sample, tm):
            return _ffn(h, mod, i, sub, per_sample, g_norm[i, sub][None],
                        ffn_w1[i, sub // 2].astype(BF16), ffn_w3[i, sub // 2].astype(BF16),
                        ffn_w2[i, sub // 2].astype(BF16), tm)

        h_x = ffn(h_x, 0, True, tm_x)
        if run_ctx_in:
            h_c = ffn(h_c, 0, False, tm_c)

        g_mix = g_norm[i, 1][None]
        if kind == 0:
            w_in = sc_w_in[j].astype(BF16)
            w_out = sc_w_out[j].astype(BF16)
            h_x = _shortconv(h_x, mod, i, True, g_mix, w_in, sc_conv[j], w_out, tm_x)
            if run_ctx_out:
                h_c = _shortconv(h_c, mod, i, False, g_mix, w_in, sc_conv[j], w_out, tm_c)
        else:
            wts = _mla_weights(mla_w_a[j], mla_g_qa[j], mla_w_uq[j], mla_g_kva[j], mla_w_ukv[j],
                               mla_g_q[j], mla_g_k[j])
            w_o = mla_w_o[j].astype(BF16)
            q_c, k_c, v_c = _mla_proj(h_c, mod, i, False, g_mix, wts, cos_c, sin_c, tm_c)
            q_x, k_x, v_x = _mla_proj(h_x, mod, i, True, g_mix, wts, cos_x, sin_x, tm_x)
            o_x = _attention(q_x, [k_c, k_x], [v_c, v_x], 512)
            h_x = _outproj(h_x, o_x, mod, i, True, w_o, tm_x)
            if run_ctx_out:
                o_c = _attention(q_c, [k_c], [v_c], n_ctx)
                h_c = _outproj(h_c, o_c, mod, i, False, w_o, tm_c)

        h_x = ffn(h_x, 2, True, tm_x)
        if run_ctx_out:
            h_c = ffn(h_c, 2, False, tm_c)
    return h_x
```

```python
import functools
import math

import jax
import jax.numpy as jnp
import numpy as np
from jax.experimental import pallas as pl
from jax.experimental.pallas import tpu as pltpu

GRID_W = 64
N_MOD = 9
MLA_HEADS = 8
QK_NOPE = 128
QK_ROPE = 64
QK_HEAD = QK_NOPE + QK_ROPE
V_HEAD = 128
Q_LORA = 256
KV_LORA = 128
ROPE_BASE = 10000.0
QK_SCALE = QK_HEAD ** -0.5
Q_FOLD = QK_SCALE * math.log2(math.e)
EPS = 1e-6

LANES = 128
QK_PAD = 2 * LANES
VMEM_LIMIT = 56 * 1024 * 1024
MOD_ROWS = 24

BF16 = jnp.bfloat16
F32 = jnp.float32


def _dot(a, b):
    return jnp.dot(a, b, preferred_element_type=F32)


def _resident(shape):
    nd = len(shape)
    return pl.BlockSpec(shape, lambda *_: (0,) * nd, pipeline_mode=pl.Buffered(1))


def _params(n_grid):
    return pltpu.CompilerParams(
        dimension_semantics=("arbitrary",) * n_grid, vmem_limit_bytes=VMEM_LIMIT)


def _pre(h, g, shift, scale):
    ms = jnp.mean(h * h, axis=-1, keepdims=True)
    return (h * jax.lax.rsqrt(ms + EPS) * g) * (1.0 + scale) + shift


def _silu(a):
    return a * jax.nn.sigmoid(a)


def _mod_kernel(cond_ref, w_ref, b_ref, o_ref):
    cond = _silu(cond_ref[...]).astype(BF16)
    o_ref[0] = _dot(cond, w_ref[0].astype(BF16)) + b_ref[0]


def _modulation(cond, w_mod, b_mod):
    depth, d, nm = w_mod.shape
    tn = nm // 4
    return pl.pallas_call(
        _mod_kernel,
        grid=(depth, nm // tn),
        in_specs=[
            pl.BlockSpec((MOD_ROWS, d), lambda i, j: (0, 0)),
            pl.BlockSpec((1, d, tn), lambda i, j: (i, 0, j)),
            pl.BlockSpec((1, 1, tn), lambda i, j: (i, 0, j)),
        ],
        out_specs=pl.BlockSpec((1, MOD_ROWS, tn), lambda i, j: (i, 0, j)),
        out_shape=jax.ShapeDtypeStruct((depth, MOD_ROWS, nm), F32),
        compiler_params=_params(2),
        name="adaln_mod",
    )(cond, w_mod, b_mod.reshape(depth, 1, nm))


def _mod_spec(layer, sub, per_sample, d):
    if per_sample:
        return pl.BlockSpec((1, 1, 1, 3, d), lambda b, t: (layer, b, sub, 0, 0))
    ctx_row = MOD_ROWS - 8
    return pl.BlockSpec((1, 1, 1, 3, d), lambda b, t: (layer, ctx_row, sub, 0, 0))


def _ffn_kernel(h_ref, mod_ref, g_ref, w1_ref, w3_ref, w2_ref, o_ref):
    h = h_ref[0]
    mod = mod_ref[0, 0, 0]
    xn = _pre(h, g_ref[...], mod[0:1], mod[1:2]).astype(BF16)
    a = _dot(xn, w1_ref[...])
    b = _dot(xn, w3_ref[...])
    gated = (_silu(a) * b).astype(BF16)
    f = _dot(gated, w2_ref[...])
    o_ref[0] = h + (0.5 * mod[2:3]) * f


def _ffn(h, mod, layer, sub, per_sample, g, w1, w3, w2, tm):
    bsz, n, d = h.shape
    dff = w1.shape[1]
    tile = pl.BlockSpec((1, tm, d), lambda b, t: (b, t, 0))
    return pl.pallas_call(
        _ffn_kernel,
        grid=(bsz, n // tm),
        in_specs=[tile, _mod_spec(layer, sub, per_sample, d), _resident((1, d)),
                  _resident((d, dff)), _resident((d, dff)), _resident((dff, d))],
        out_specs=tile,
        out_shape=jax.ShapeDtypeStruct(h.shape, F32),
        compiler_params=_params(2),
        name="ffn",
    )(h, mod, g, w1, w3, w2)


def _shortconv_kernel(h_ref, hp_ref, hn_ref, mod_ref, g_ref, win_ref, cw_ref, wout_ref, o_ref):
    t = pl.program_id(1)
    nt = pl.num_programs(1)
    h = h_ref[0]
    tm, d = h.shape
    mod = mod_ref[0, 0, 0]
    g = g_ref[...]
    xn = _pre(h, g, mod[0:1], mod[1:2]).astype(BF16)
    p = _dot(xn, win_ref[...])
    b_gate = p[:, :d]
    cu = p[:, d:2 * d] * p[:, 2 * d:]

    halo = jnp.concatenate([hp_ref[0], hn_ref[0]], axis=0)
    hx = _pre(halo, g, mod[0:1], mod[1:2]).astype(BF16)
    ph = _dot(hx, win_ref[:, d:])
    cu_halo = ph[:, :d] * ph[:, d:]
    prev_row = jnp.where(t > 0, cu_halo[7:8], 0.0)
    next_row = jnp.where(t < nt - 1, cu_halo[8:9], 0.0)

    row = jax.lax.broadcasted_iota(jnp.int32, (tm, d), 0)
    cu_prev = jnp.where(row == 0, prev_row, pltpu.roll(cu, 1, 0))
    cu_next = jnp.where(row == tm - 1, next_row, pltpu.roll(cu, tm - 1, 0))
    cw = cw_ref[...]
    conv = cw[0:1] * cu_prev + cw[1:2] * cu + cw[2:3] * cu_next
    y = (b_gate * conv).astype(BF16)
    o_ref[0] = h + mod[2:3] * _dot(y, wout_ref[...])


def _shortconv(h, mod, layer, per_sample, g, w_in, conv_w, w_out, tm):
    bsz, n, d = h.shape
    tile = pl.BlockSpec((1, tm, d), lambda b, t: (b, t, 0))
    r = tm // 8
    last = n // 8 - 1
    prev = pl.BlockSpec((1, 8, d), lambda b, t: (b, jnp.maximum(t * r - 1, 0), 0))
    nxt = pl.BlockSpec((1, 8, d), lambda b, t: (b, jnp.minimum((t + 1) * r, last), 0))
    return pl.pallas_call(
        _shortconv_kernel,
        grid=(bsz, n // tm),
        in_specs=[tile, prev, nxt, _mod_spec(layer, 1, per_sample, d), _resident((1, d)),
                  _resident(w_in.shape), _resident(conv_w.shape), _resident(w_out.shape)],
        out_specs=tile,
        out_shape=jax.ShapeDtypeStruct(h.shape, F32),
        compiler_params=_params(2),
        name="shortconv",
    )(h, h, h, mod, g, w_in, conv_w, w_out)


def _mla_proj_kernel(h_ref, mod_ref, g_ref, wa_ref, gqa_ref, wuq_ref, gkva_ref, wukv_ref,
                     gq_ref, gk_ref, cos_ref, sin_ref, q_ref, k_ref, v_ref):
    h = h_ref[0]
    mod = mod_ref[0, 0, 0]
    xn = _pre(h, g_ref[...], mod[0:1], mod[1:2]).astype(BF16)
    a = _dot(xn, wa_ref[...])
    cq = a[:, :Q_LORA]
    ckv = a[:, Q_LORA:Q_LORA + KV_LORA]
    kr = a[:, Q_LORA + KV_LORA:Q_LORA + KV_LORA + LANES]
    krs = a[:, Q_LORA + KV_LORA + LANES:]

    def rms(x, gain):
        return x * jax.lax.rsqrt(jnp.mean(x * x, axis=-1, keepdims=True) + EPS) * gain

    qraw = _dot(rms(cq, gqa_ref[...]).astype(BF16), wuq_ref[...])
    kvraw = _dot(rms(ckv, gkva_ref[...]).astype(BF16), wukv_ref[...])

    cos = cos_ref[...]
    sin = sin_ref[...]
    gq = gq_ref[...]
    gk = gk_ref[...]
    hw = MLA_HEADS * LANES
    kr_rot = kr * (cos * gk[1:2]) + krs * (sin * gk[2:3])
    kr_ss = jnp.sum(kr * kr, axis=-1, keepdims=True)
    q_cos = cos * gq[1:2]
    q_sin = sin * gq[2:3]
    for hd in range(MLA_HEADS):
        lo = hd * LANES
        qn = qraw[:, lo:lo + LANES]
        qr = qraw[:, hw + lo:hw + lo + LANES]
        qrs = qraw[:, 2 * hw + lo:2 * hw + lo + LANES]
        ss = jnp.sum(qn * qn, axis=-1, keepdims=True) + jnp.sum(qr * qr, axis=-1, keepdims=True)
        r = jax.lax.rsqrt(ss * (1.0 / QK_HEAD) + EPS) * Q_FOLD
        q_ref[0, hd, :, :LANES] = (qn * gq[0:1] * r).astype(BF16)
        q_ref[0, hd, :, LANES:] = ((qr * q_cos + qrs * q_sin) * r).astype(BF16)

        kn = kvraw[:, lo:lo + LANES]
        ssk = jnp.sum(kn * kn, axis=-1, keepdims=True) + kr_ss
        rk = jax.lax.rsqrt(ssk * (1.0 / QK_HEAD) + EPS)
        k_ref[0, hd, :, :LANES] = (kn * gk[0:1] * rk).astype(BF16)
        k_ref[0, hd, :, LANES:] = (kr_rot * rk).astype(BF16)
        v_ref[0, hd] = kvraw[:, hw + lo:hw + lo + LANES].astype(BF16)


def _mla_proj(h, mod, layer, per_sample, g, wts, cos, sin, tm):
    bsz, n, d = h.shape
    tile = pl.BlockSpec((1, tm, d), lambda b, t: (b, t, 0))
    tab = pl.BlockSpec((tm, LANES), lambda b, t: (t, 0))
    qk_spec = pl.BlockSpec((1, MLA_HEADS, tm, QK_PAD), lambda b, t: (b, 0, t, 0))
    v_spec = pl.BlockSpec((1, MLA_HEADS, tm, V_HEAD), lambda b, t: (b, 0, t, 0))
    qk_shape = jax.ShapeDtypeStruct((bsz, MLA_HEADS, n, QK_PAD), BF16)
    v_shape = jax.ShapeDtypeStruct((bsz, MLA_HEADS, n, V_HEAD), BF16)
    small = [wts["w_a"], wts["g_qa"], wts["w_uq"], wts["g_kva"], wts["w_ukv"], wts["g_q"], wts["g_k"]]
    return pl.pallas_call(
        _mla_proj_kernel,
        grid=(bsz, n // tm),
        in_specs=[tile, _mod_spec(layer, 1, per_sample, d), _resident((1, d))]
                 + [_resident(w.shape) for w in small] + [tab, tab],
        out_specs=[qk_spec, qk_spec, v_spec],
        out_shape=[qk_shape, qk_shape, v_shape],
        compiler_params=_params(2),
        name="mla_proj",
    )(h, mod, g, *small, cos, sin)


_CONTRACT_LAST = (((1,), (1,)), ((), ()))
_CONTRACT_FIRST = (((0,), (0,)), ((), ()))


def _attn_kernel(*refs, n_parts, chunk):
    q_ref = refs[0]
    ks = [r[0, 0] for r in refs[1:1 + n_parts]]
    vs = [r[0, 0] for r in refs[1 + n_parts:1 + 2 * n_parts]]
    o_ref = refs[1 + 2 * n_parts]
    tq = q_ref.shape[2]
    for c in range(tq // chunk):
        cols = slice(c * chunk, (c + 1) * chunk)
        q = q_ref[0, 0, cols, :]
        s = [jax.lax.dot_general(k, q, _CONTRACT_LAST, preferred_element_type=F32) for k in ks]
        m = functools.reduce(jnp.maximum, [jnp.max(x, axis=0, keepdims=True) for x in s])
        p = [jnp.exp2(x - m) for x in s]
        denom = functools.reduce(jnp.add, [jnp.sum(x, axis=0, keepdims=True) for x in p])
        o = functools.reduce(jnp.add, [
            jax.lax.dot_general(v, x.astype(BF16), _CONTRACT_FIRST, preferred_element_type=F32)
            for x, v in zip(p, vs)])
        o_ref[0, :, cols] = (o / denom).astype(o_ref.dtype)


def _attention(q, ks, vs, tq, chunk):
    bsz, nh, n, _ = q.shape
    n_parts = len(ks)
    q_spec = pl.BlockSpec((1, 1, tq, QK_PAD), lambda b, h, t: (b, h, t, 0))
    k_specs = [pl.BlockSpec((1, 1, k.shape[2], QK_PAD), lambda b, h, t: (b, h, 0, 0)) for k in ks]
    v_specs = [pl.BlockSpec((1, 1, v.shape[2], V_HEAD), lambda b, h, t: (b, h, 0, 0)) for v in vs]
    return pl.pallas_call(
        functools.partial(_attn_kernel, n_parts=n_parts, chunk=chunk),
        grid=(bsz, nh, n // tq),
        in_specs=[q_spec] + k_specs + v_specs,
        out_specs=pl.BlockSpec((1, V_HEAD, tq), lambda b, h, t: (b, h, t)),
        out_shape=jax.ShapeDtypeStruct((bsz, nh * V_HEAD, n), BF16),
        compiler_params=_params(3),
        name="attention",
    )(q, *ks, *vs)


def _outproj_kernel(h_ref, o_ref, mod_ref, wo_ref, out_ref):
    mod = mod_ref[0, 0, 0]
    proj = jax.lax.dot_general(o_ref[0], wo_ref[...], _CONTRACT_FIRST, preferred_element_type=F32)
    out_ref[0] = h_ref[0] + mod[2:3] * proj


def _outproj(h, o, mod, layer, per_sample, w_o, tm):
    bsz, n, d = h.shape
    tile = pl.BlockSpec((1, tm, d), lambda b, t: (b, t, 0))
    o_tile = pl.BlockSpec((1, o.shape[1], tm), lambda b, t: (b, 0, t))
    return pl.pallas_call(
        _outproj_kernel,
        grid=(bsz, n // tm),
        in_specs=[tile, o_tile, _mod_spec(layer, 1, per_sample, d), _resident(w_o.shape)],
        out_specs=tile,
        out_shape=jax.ShapeDtypeStruct(h.shape, F32),
        compiler_params=_params(2),
        name="attn_outproj",
    )(h, o, mod, w_o)


_SWAP = np.concatenate([np.arange(16, 32), np.arange(0, 16), np.arange(48, 64), np.arange(32, 48)])


def _pad_lanes(w):
    return jnp.concatenate([w, jnp.zeros_like(w)], axis=-1)


def _mla_weights(w_a, g_qa, w_uq, g_kva, w_ukv, g_q, g_k):
    d = w_a.shape[0]
    kr = w_a[:, Q_LORA + KV_LORA:]
    w_a2 = jnp.concatenate(
        [w_a[:, :Q_LORA + KV_LORA], _pad_lanes(kr), _pad_lanes(kr[:, _SWAP])], axis=1)
    uq = w_uq.reshape(Q_LORA, MLA_HEADS, QK_HEAD)
    uq_rope = uq[:, :, QK_NOPE:]
    w_uq2 = jnp.concatenate(
        [uq[:, :, :QK_NOPE].reshape(Q_LORA, -1),
         _pad_lanes(uq_rope).reshape(Q_LORA, -1),
         _pad_lanes(uq_rope[:, :, _SWAP]).reshape(Q_LORA, -1)], axis=1)
    ukv = w_ukv.reshape(KV_LORA, MLA_HEADS, QK_NOPE + V_HEAD)
    w_ukv2 = jnp.concatenate(
        [ukv[:, :, :QK_NOPE].reshape(KV_LORA, -1), ukv[:, :, QK_NOPE:].reshape(KV_LORA, -1)], axis=1)

    def gains(gv):
        rope = gv[QK_NOPE:]
        return jnp.stack([gv[:QK_NOPE], _pad_lanes(rope), _pad_lanes(rope[_SWAP])])

    del d
    return {
        "w_a": w_a2.astype(BF16), "g_qa": g_qa[None], "w_uq": w_uq2.astype(BF16),
        "g_kva": g_kva[None], "w_ukv": w_ukv2.astype(BF16), "g_q": gains(g_q), "g_k": gains(g_k),
    }


def _rope_tables(n):
    pos = np.arange(n)
    n_freq = QK_ROPE // 4
    inv = ROPE_BASE ** (-np.arange(n_freq, dtype=np.float64) / n_freq)
    ang_r = (pos // GRID_W).astype(np.float64)[:, None] * inv
    ang_c = (pos % GRID_W).astype(np.float64)[:, None] * inv
    zeros = np.zeros((n, LANES - QK_ROPE), np.float64)
    cos = np.concatenate([np.cos(ang_r), np.cos(ang_r), np.cos(ang_c), np.cos(ang_c), zeros], axis=1)
    sin = np.concatenate([-np.sin(ang_r), np.sin(ang_r), -np.sin(ang_c), np.sin(ang_c), zeros], axis=1)
    return jnp.asarray(cos, F32), jnp.asarray(sin, F32)


def _identity_tables(n):
    ones = np.concatenate([np.ones((n, QK_ROPE), np.float32), np.zeros((n, LANES - QK_ROPE), np.float32)], axis=1)
    return jnp.asarray(ones, F32), jnp.zeros((n, LANES), F32)


def kernel(x, c, ctx, c_ctx, w_mod, b_mod, g_norm, ffn_w1, ffn_w3, ffn_w2, sc_w_in, sc_conv, sc_w_out,
           mla_w_a, mla_g_qa, mla_w_uq, mla_g_kva, mla_w_ukv, mla_g_q, mla_g_k, mla_w_o):
    bsz, n, d = x.shape
    n_ctx = ctx.shape[1]
    depth = w_mod.shape[0]
    tm_x = 512
    tm_c = n_ctx

    cond = jnp.concatenate([c, c_ctx[None], jnp.zeros((MOD_ROWS - bsz - 1, d), F32)], axis=0)
    mod = _modulation(cond, w_mod, b_mod).reshape(depth, MOD_ROWS, 3, 3, d)

    cos_x, sin_x = _rope_tables(n)
    cos_c, sin_c = _identity_tables(n_ctx)

    h_x, h_c = x, ctx
    for i in range(depth):
        kind, j = i % 2, i // 2
        last = i == depth - 1
        run_ctx_in = (not last) or kind == 1
        run_ctx_out = not last

        def ffn(h, sub, per_sample, tm):
            return _ffn(h, mod, i, sub, per_sample, g_norm[i, sub][None],
                        ffn_w1[i, sub // 2].astype(BF16), ffn_w3[i, sub // 2].astype(BF16),
                        ffn_w2[i, sub // 2].astype(BF16), tm)

        def ffn_ctx(h, sub):
            return ffn(h.reshape(1, bsz * n_ctx, d), sub, False, tm_x).reshape(bsz, n_ctx, d)

        h_x = ffn(h_x, 0, True, tm_x)
        if run_ctx_in:
            h_c = ffn_ctx(h_c, 0)

        g_mix = g_norm[i, 1][None]
        if kind == 0:
            w_in = sc_w_in[j].astype(BF16)
            w_out = sc_w_out[j].astype(BF16)
            h_x = _shortconv(h_x, mod, i, True, g_mix, w_in, sc_conv[j], w_out, tm_x)
            if run_ctx_out:
                h_c = _shortconv(h_c, mod, i, False, g_mix, w_in, sc_conv[j], w_out, tm_c)
        else:
            wts = _mla_weights(mla_w_a[j], mla_g_qa[j], mla_w_uq[j], mla_g_kva[j], mla_w_ukv[j],
                               mla_g_q[j], mla_g_k[j])
            w_o = mla_w_o[j].astype(BF16)
            q_c, k_c, v_c = _mla_proj(h_c, mod, i, False, g_mix, wts, cos_c, sin_c, tm_c)
            q_x, k_x, v_x = _mla_proj(h_x, mod, i, True, g_mix, wts, cos_x, sin_x, tm_x)
            o_x = _attention(q_x, [k_c, k_x], [v_c, v_x], 1024, 512)
            h_x = _outproj(h_x, o_x, mod, i, True, w_o, tm_x)
            if run_ctx_out:
                o_c = _attention(q_c, [k_c], [v_c], n_ctx, n_ctx)
                h_c = _outproj(h_c, o_c, mod, i, False, w_o, tm_c)

        h_x = ffn(h_x, 2, True, tm_x)
        if run_ctx_out:
            h_c = ffn_ctx(h_c, 2)
    return h_x
```

```python
import functools
import math

import jax
import jax.numpy as jnp
import numpy as np
from jax.experimental import pallas as pl
from jax.experimental.pallas import tpu as pltpu

GRID_W = 64
N_MOD = 9
MLA_HEADS = 8
QK_NOPE = 128
QK_ROPE = 64
QK_HEAD = QK_NOPE + QK_ROPE
V_HEAD = 128
Q_LORA = 256
KV_LORA = 128
ROPE_BASE = 10000.0
QK_SCALE = QK_HEAD ** -0.5
Q_FOLD = QK_SCALE * math.log2(math.e)
EPS = 1e-6

LANES = 128
SUBLANES = 8
QK_PAD = 2 * LANES
VMEM_LIMIT = 56 * 1024 * 1024
MOD_ROWS = 24

FFN_TILE = 1024
SUB = 512
PROJ_TILE = 512
ATTN_CHUNK = 512

BF16 = jnp.bfloat16
F32 = jnp.float32

_CONTRACT_LAST = (((1,), (1,)), ((), ()))
_CONTRACT_FIRST = (((0,), (0,)), ((), ()))


def _dot(a, b):
    return jnp.dot(a, b, preferred_element_type=F32)


def _dot_t(a_t, b):
    return jax.lax.dot_general(a_t, b, _CONTRACT_FIRST, preferred_element_type=F32)


def _resident(shape):
    nd = len(shape)
    return pl.BlockSpec(shape, lambda *_: (0,) * nd, pipeline_mode=pl.Buffered(1))


def _stacked(shape, lead):
    nl = len(lead)
    tail = len(shape) - nl
    return pl.BlockSpec((1,) * nl + tuple(shape[nl:]), lambda *_: tuple(lead) + (0,) * tail,
                        pipeline_mode=pl.Buffered(1))


def _params(n_grid):
    return pltpu.CompilerParams(
        dimension_semantics=("arbitrary",) * n_grid, vmem_limit_bytes=VMEM_LIMIT)


def _pre(h, gain_scale, shift):
    ms = jnp.mean(h * h, axis=-1, keepdims=True)
    return (h * jax.lax.rsqrt(ms + EPS)) * gain_scale + shift


def _silu(a):
    return a * jax.nn.sigmoid(a)


def _mod_kernel(cond_ref, w_ref, b_ref, o_ref):
    cond = _silu(cond_ref[...]).astype(BF16)
    o_ref[0] = _dot(cond, w_ref[0].astype(BF16)) + b_ref[0]


def _modulation(cond, w_mod, b_mod):
    depth, d, nm = w_mod.shape
    tn = nm // 4
    return pl.pallas_call(
        _mod_kernel,
        grid=(depth, nm // tn),
        in_specs=[
            pl.BlockSpec((MOD_ROWS, d), lambda i, j: (0, 0)),
            pl.BlockSpec((1, d, tn), lambda i, j: (i, 0, j)),
            pl.BlockSpec((1, 1, tn), lambda i, j: (i, 0, j)),
        ],
        out_specs=pl.BlockSpec((1, MOD_ROWS, tn), lambda i, j: (i, 0, j)),
        out_shape=jax.ShapeDtypeStruct((depth, MOD_ROWS, nm), F32),
        compiler_params=_params(2),
        name="adaln_mod",
    )(cond, w_mod, b_mod.reshape(depth, 1, nm))


def _mod_spec(layer, sub, per_sample, d):
    if per_sample:
        return pl.BlockSpec((1, 1, 1, 3, d), lambda b, t: (layer, b, sub, 0, 0))
    ctx_row = MOD_ROWS - SUBLANES
    return pl.BlockSpec((1, 1, 1, 3, d), lambda b, t: (layer, ctx_row, sub, 0, 0))


def _gain_spec(layer, sub, d):
    return pl.BlockSpec((1, 1, d), lambda b, t: (3 * layer + sub, 0, 0))


def _ffn_kernel(*refs, fused_outproj):
    if fused_outproj:
        h_ref, ot_ref, mix_ref, wo_ref, mod_ref, g_ref, w1_ref, w3_ref, w2_ref, o_ref = refs
    else:
        h_ref, mod_ref, g_ref, w1_ref, w3_ref, w2_ref, o_ref = refs
    mod = mod_ref[0, 0, 0]
    gain_scale = g_ref[0] * (1.0 + mod[1:2])
    shift = mod[0:1]
    half_gate = 0.5 * mod[2:3]
    tm = h_ref.shape[1]
    sub = min(SUB, tm)
    for s in range(tm // sub):
        rows = slice(s * sub, (s + 1) * sub)
        h = h_ref[0, rows, :]
        if fused_outproj:
            h = h + mix_ref[0, 0, 0][2:3] * _dot_t(ot_ref[0, :, rows], wo_ref[0])
        xn = _pre(h, gain_scale, shift).astype(BF16)
        a = _dot(xn, w1_ref[0, 0])
        b = _dot(xn, w3_ref[0, 0])
        gated = (_silu(a) * b).astype(BF16)
        o_ref[0, rows, :] = h + half_gate * _dot(gated, w2_ref[0, 0])


def _ffn(h, mod, layer, sub, per_sample, g_all, w1_all, w3_all, w2_all, tm, outproj=None):
    bsz, n, d = h.shape
    tile = pl.BlockSpec((1, tm, d), lambda b, t: (b, t, 0))
    which = sub // 2
    specs = [_mod_spec(layer, sub, per_sample, d), _gain_spec(layer, sub, d),
             _stacked(w1_all.shape, (layer, which)), _stacked(w3_all.shape, (layer, which)),
             _stacked(w2_all.shape, (layer, which))]
    args = [mod, g_all, w1_all, w3_all, w2_all]
    if outproj is not None:
        o_t, w_o_all, j = outproj
        specs = [pl.BlockSpec((1, o_t.shape[1], tm), lambda b, t: (b, 0, t)),
                 _mod_spec(layer, 1, per_sample, d), _stacked(w_o_all.shape, (j,))] + specs
        args = [o_t, mod, w_o_all] + args
    return pl.pallas_call(
        functools.partial(_ffn_kernel, fused_outproj=outproj is not None),
        grid=(bsz, n // tm),
        in_specs=[tile] + specs,
        out_specs=tile,
        out_shape=jax.ShapeDtypeStruct(h.shape, F32),
        compiler_params=_params(2),
        name="ffn",
    )(h, *args)


def _shortconv_kernel(h_ref, mod_ref, g_ref, win_ref, cw_ref, wout_ref, o_ref, cu_scr):
    n, d = h_ref.shape[1], h_ref.shape[2]
    sub = min(SUB, n)
    n_sub = n // sub
    mod = mod_ref[0, 0, 0]
    gain_scale = g_ref[0] * (1.0 + mod[1:2])
    shift = mod[0:1]
    gate = mod[2:3]
    cw = cw_ref[0]
    zero_row = jnp.zeros((1, d), F32)
    cu_scr[SUBLANES - 1:SUBLANES, :] = zero_row
    cu_scr[n + SUBLANES:n + SUBLANES + 1, :] = zero_row

    def project(s):
        rows = slice(s * sub, (s + 1) * sub)
        xn = _pre(h_ref[0, rows, :], gain_scale, shift).astype(BF16)
        p = _dot(xn, win_ref[0])
        cu_scr[SUBLANES + s * sub:SUBLANES + (s + 1) * sub, :] = p[:, d:2 * d] * p[:, 2 * d:]
        return p[:, :d]

    def mix(s, b_gate):
        rows = slice(s * sub, (s + 1) * sub)
        lo = SUBLANES + s * sub
        conv = (cw[0:1] * cu_scr[lo - 1:lo - 1 + sub, :] + cw[1:2] * cu_scr[lo:lo + sub, :]
                + cw[2:3] * cu_scr[lo + 1:lo + 1 + sub, :])
        y = (b_gate * conv).astype(BF16)
        o_ref[0, rows, :] = h_ref[0, rows, :] + gate * _dot(y, wout_ref[0])

    b_gate = project(0)
    for s in range(n_sub):
        b_next = project(s + 1) if s + 1 < n_sub else None
        mix(s, b_gate)
        b_gate = b_next


def _shortconv(h, mod, layer, per_sample, g_all, w_in_all, conv_all, w_out_all, j):
    bsz, n, d = h.shape
    tile = pl.BlockSpec((1, n, d), lambda b, t: (b, 0, 0))
    return pl.pallas_call(
        _shortconv_kernel,
        grid=(bsz, 1),
        in_specs=[tile, _mod_spec(layer, 1, per_sample, d), _gain_spec(layer, 1, d),
                  _stacked(w_in_all.shape, (j,)), _stacked(conv_all.shape, (j,)),
                  _stacked(w_out_all.shape, (j,))],
        out_specs=tile,
        out_shape=jax.ShapeDtypeStruct(h.shape, F32),
        scratch_shapes=[pltpu.VMEM((n + 2 * SUBLANES, d), F32)],
        compiler_params=_params(2),
        name="shortconv",
    )(h, mod, g_all, w_in_all, conv_all, w_out_all)


def _mla_proj_kernel(h_ref, mod_ref, g_ref, wa_ref, gqa_ref, wuq_ref, gkva_ref, wukv_ref,
                     gq_ref, gk_ref, cos_ref, sin_ref, q_ref, k_ref, v_ref):
    h = h_ref[0]
    mod = mod_ref[0, 0, 0]
    xn = _pre(h, g_ref[0] * (1.0 + mod[1:2]), mod[0:1]).astype(BF16)
    a = _dot(xn, wa_ref[...])
    cq = a[:, :Q_LORA]
    ckv = a[:, Q_LORA:Q_LORA + KV_LORA]
    kr = a[:, Q_LORA + KV_LORA:Q_LORA + KV_LORA + LANES]
    krs = a[:, Q_LORA + KV_LORA + LANES:]

    def rms(x, gain):
        return x * jax.lax.rsqrt(jnp.mean(x * x, axis=-1, keepdims=True) + EPS) * gain

    qraw = _dot(rms(cq, gqa_ref[...]).astype(BF16), wuq_ref[...])
    kvraw = _dot(rms(ckv, gkva_ref[...]).astype(BF16), wukv_ref[...])

    cos = cos_ref[...]
    sin = sin_ref[...]
    gq = gq_ref[...]
    gk = gk_ref[...]
    hw = MLA_HEADS * LANES
    nope_gain = gq[0:1] * gk[0:1]
    kr_rot = kr * (cos * gk[1:2]) + krs * (sin * gk[2:3])
    kr_ss = jnp.sum(kr * kr, axis=-1, keepdims=True)
    q_cos = cos * gq[1:2]
    q_sin = sin * gq[2:3]
    for hd in range(MLA_HEADS):
        lo = hd * LANES
        qn = qraw[:, lo:lo + LANES]
        qr = qraw[:, hw + lo:hw + lo + LANES]
        qrs = qraw[:, 2 * hw + lo:2 * hw + lo + LANES]
        ss = jnp.sum(qn * qn, axis=-1, keepdims=True) + jnp.sum(qr * qr, axis=-1, keepdims=True)
        r = jax.lax.rsqrt(ss * (1.0 / QK_HEAD) + EPS) * Q_FOLD
        q_ref[0, hd, :, :LANES] = (qn * r).astype(BF16)
        q_ref[0, hd, :, LANES:] = ((qr * q_cos + qrs * q_sin) * r).astype(BF16)

        kn = kvraw[:, lo:lo + LANES]
        ssk = jnp.sum(kn * kn, axis=-1, keepdims=True) + kr_ss
        rk = jax.lax.rsqrt(ssk * (1.0 / QK_HEAD) + EPS)
        k_ref[0, hd, :, :LANES] = (kn * nope_gain * rk).astype(BF16)
        k_ref[0, hd, :, LANES:] = (kr_rot * rk).astype(BF16)
        v_ref[0, hd] = kvraw[:, hw + lo:hw + lo + LANES].astype(BF16)


def _mla_proj(h, mod, layer, per_sample, g_all, wts, cos, sin, tm):
    bsz, n, d = h.shape
    tile = pl.BlockSpec((1, tm, d), lambda b, t: (b, t, 0))
    tab = pl.BlockSpec((tm, LANES), lambda b, t: (t, 0))
    qk_spec = pl.BlockSpec((1, MLA_HEADS, tm, QK_PAD), lambda b, t: (b, 0, t, 0))
    v_spec = pl.BlockSpec((1, MLA_HEADS, tm, V_HEAD), lambda b, t: (b, 0, t, 0))
    qk_shape = jax.ShapeDtypeStruct((bsz, MLA_HEADS, n, QK_PAD), BF16)
    v_shape = jax.ShapeDtypeStruct((bsz, MLA_HEADS, n, V_HEAD), BF16)
    small = [wts["w_a"], wts["g_qa"], wts["w_uq"], wts["g_kva"], wts["w_ukv"], wts["g_q"], wts["g_k"]]
    return pl.pallas_call(
        _mla_proj_kernel,
        grid=(bsz, n // tm),
        in_specs=[tile, _mod_spec(layer, 1, per_sample, d), _gain_spec(layer, 1, d)]
                 + [_resident(w.shape) for w in small] + [tab, tab],
        out_specs=[qk_spec, qk_spec, v_spec],
        out_shape=[qk_shape, qk_shape, v_shape],
        compiler_params=_params(2),
        name="mla_proj",
    )(h, mod, g_all, *small, cos, sin)


def _attn_kernel(*refs, n_parts):
    q_ref = refs[0]
    k_refs = refs[1:1 + n_parts]
    v_refs = refs[1 + n_parts:1 + 2 * n_parts]
    o_ref = refs[1 + 2 * n_parts]
    s_scr = refs[2 + 2 * n_parts]
    n_slots, _, chunk = s_scr.shape
    n_chunks = q_ref.shape[2] // chunk
    sizes = [k.shape[2] for k in k_refs]
    offs = [sum(sizes[:i]) for i in range(n_parts)]

    def scores(c):
        q = q_ref[0, 0, c * chunk:(c + 1) * chunk, :]
        col_max = None
        for k_ref, off, sz in zip(k_refs, offs, sizes):
            s = jax.lax.dot_general(k_ref[0, 0], q, _CONTRACT_LAST, preferred_element_type=F32)
            s_scr[c % n_slots, off:off + sz, :] = s
            part = jnp.max(s, axis=0, keepdims=True)
            col_max = part if col_max is None else jnp.maximum(col_max, part)
        return col_max

    def attend(c, col_max):
        denom = None
        out = None
        for v_ref, off, sz in zip(v_refs, offs, sizes):
            p = jnp.exp2(s_scr[c % n_slots, off:off + sz, :] - col_max)
            part = jnp.sum(p, axis=0, keepdims=True)
            denom = part if denom is None else denom + part
            acc = _dot_t(v_ref[0, 0], p.astype(BF16))
            out = acc if out is None else out + acc
        o_ref[0, :, c * chunk:(c + 1) * chunk] = (out * (1.0 / denom)).astype(o_ref.dtype)

    col_max = scores(0)
    for c in range(n_chunks):
        next_max = scores(c + 1) if c + 1 < n_chunks else None
        attend(c, col_max)
        col_max = next_max


def _attention(q, ks, vs):
    bsz, nh, n, _ = q.shape
    n_parts = len(ks)
    chunk = min(ATTN_CHUNK, n)
    n_keys = sum(k.shape[2] for k in ks)
    q_spec = pl.BlockSpec((1, 1, n, QK_PAD), lambda b, h: (b, h, 0, 0))
    k_specs = [pl.BlockSpec((1, 1, k.shape[2], QK_PAD), lambda b, h: (b, h, 0, 0)) for k in ks]
    v_specs = [pl.BlockSpec((1, 1, v.shape[2], V_HEAD), lambda b, h: (b, h, 0, 0)) for v in vs]
    return pl.pallas_call(
        functools.partial(_attn_kernel, n_parts=n_parts),
        grid=(bsz, nh),
        in_specs=[q_spec] + k_specs + v_specs,
        out_specs=pl.BlockSpec((1, V_HEAD, n), lambda b, h: (b, h, 0)),
        out_shape=jax.ShapeDtypeStruct((bsz, nh * V_HEAD, n), BF16),
        scratch_shapes=[pltpu.VMEM((min(2, n // chunk), n_keys, chunk), F32)],
        compiler_params=_params(2),
        name="attention",
    )(q, *ks, *vs)


def _outproj_kernel(h_ref, o_ref, mod_ref, wo_ref, out_ref):
    mod = mod_ref[0, 0, 0]
    out_ref[0] = h_ref[0] + mod[2:3] * _dot_t(o_ref[0], wo_ref[0])


def _outproj(h, o, mod, layer, per_sample, w_o_all, j, tm):
    bsz, n, d = h.shape
    tile = pl.BlockSpec((1, tm, d), lambda b, t: (b, t, 0))
    o_tile = pl.BlockSpec((1, o.shape[1], tm), lambda b, t: (b, 0, t))
    return pl.pallas_call(
        _outproj_kernel,
        grid=(bsz, n // tm),
        in_specs=[tile, o_tile, _mod_spec(layer, 1, per_sample, d), _stacked(w_o_all.shape, (j,))],
        out_specs=tile,
        out_shape=jax.ShapeDtypeStruct(h.shape, F32),
        compiler_params=_params(2),
        name="attn_outproj",
    )(h, o, mod, w_o_all)


_SWAP = np.concatenate([np.arange(16, 32), np.arange(0, 16), np.arange(48, 64), np.arange(32, 48)])


def _pad_lanes(w):
    return jnp.concatenate([w, jnp.zeros_like(w)], axis=-1)


def _mla_weights(w_a, g_qa, w_uq, g_kva, w_ukv, g_q, g_k):
    kr = w_a[:, Q_LORA + KV_LORA:]
    w_a2 = jnp.concatenate(
        [w_a[:, :Q_LORA + KV_LORA], _pad_lanes(kr), _pad_lanes(kr[:, _SWAP])], axis=1)
    uq = w_uq.reshape(Q_LORA, MLA_HEADS, QK_HEAD)
    uq_rope = uq[:, :, QK_NOPE:]
    w_uq2 = jnp.concatenate(
        [uq[:, :, :QK_NOPE].reshape(Q_LORA, -1),
         _pad_lanes(uq_rope).reshape(Q_LORA, -1),
         _pad_lanes(uq_rope[:, :, _SWAP]).reshape(Q_LORA, -1)], axis=1)
    ukv = w_ukv.reshape(KV_LORA, MLA_HEADS, QK_NOPE + V_HEAD)
    w_ukv2 = jnp.concatenate(
        [ukv[:, :, :QK_NOPE].reshape(KV_LORA, -1), ukv[:, :, QK_NOPE:].reshape(KV_LORA, -1)], axis=1)

    def gains(gv):
        rope = gv[QK_NOPE:]
        return jnp.stack([gv[:QK_NOPE], _pad_lanes(rope), _pad_lanes(rope[_SWAP])])

    return {
        "w_a": w_a2.astype(BF16), "g_qa": g_qa[None], "w_uq": w_uq2.astype(BF16),
        "g_kva": g_kva[None], "w_ukv": w_ukv2.astype(BF16), "g_q": gains(g_q), "g_k": gains(g_k),
    }


def _rope_tables(n):
    pos = np.arange(n)
    n_freq = QK_ROPE // 4
    inv = ROPE_BASE ** (-np.arange(n_freq, dtype=np.float64) / n_freq)
    ang_r = (pos // GRID_W).astype(np.float64)[:, None] * inv
    ang_c = (pos % GRID_W).astype(np.float64)[:, None] * inv
    zeros = np.zeros((n, LANES - QK_ROPE), np.float64)
    cos = np.concatenate([np.cos(ang_r), np.cos(ang_r), np.cos(ang_c), np.cos(ang_c), zeros], axis=1)
    sin = np.concatenate([-np.sin(ang_r), np.sin(ang_r), -np.sin(ang_c), np.sin(ang_c), zeros], axis=1)
    return jnp.asarray(cos, F32), jnp.asarray(sin, F32)


def _identity_tables(n):
    ones = np.concatenate([np.ones((n, QK_ROPE), np.float32), np.zeros((n, LANES - QK_ROPE), np.float32)], axis=1)
    return jnp.asarray(ones, F32), jnp.zeros((n, LANES), F32)


def kernel(x, c, ctx, c_ctx, w_mod, b_mod, g_norm, ffn_w1, ffn_w3, ffn_w2, sc_w_in, sc_conv, sc_w_out,
           mla_w_a, mla_g_qa, mla_w_uq, mla_g_kva, mla_w_ukv, mla_g_q, mla_g_k, mla_w_o):
    bsz, n, d = x.shape
    n_ctx = ctx.shape[1]
    depth = w_mod.shape[0]

    cond = jnp.concatenate([c, c_ctx[None], jnp.zeros((MOD_ROWS - bsz - 1, d), F32)], axis=0)
    mod = _modulation(cond, w_mod, b_mod).reshape(depth, MOD_ROWS, 3, 3, d)

    g_all = g_norm.reshape(depth * 3, 1, d)
    w1_all, w3_all, w2_all = ffn_w1.astype(BF16), ffn_w3.astype(BF16), ffn_w2.astype(BF16)
    sc_in_all, sc_out_all = sc_w_in.astype(BF16), sc_w_out.astype(BF16)
    w_o_all = mla_w_o.astype(BF16)
    cos_x, sin_x = _rope_tables(n)
    cos_c, sin_c = _identity_tables(n_ctx)

    def ffn(h, layer, sub, per_sample, outproj=None):
        return _ffn(h, mod, layer, sub, per_sample, g_all, w1_all, w3_all, w2_all,
                    min(FFN_TILE, h.shape[1]), outproj)

    def ffn_ctx(h, layer, sub):
        return ffn(h.reshape(1, bsz * n_ctx, d), layer, sub, False).reshape(bsz, n_ctx, d)

    h_x, h_c = x, ctx
    for i in range(depth):
        kind, j = i % 2, i // 2
        last = i == depth - 1
        run_ctx_in = (not last) or kind == 1
        run_ctx_out = not last

        h_x = ffn(h_x, i, 0, True)
        if run_ctx_in:
            h_c = ffn_ctx(h_c, i, 0)

        pending_outproj = None
        if kind == 0:
            h_x = _shortconv(h_x, mod, i, True, g_all, sc_in_all, sc_conv, sc_out_all, j)
            if run_ctx_out:
                h_c = _shortconv(h_c, mod, i, False, g_all, sc_in_all, sc_conv, sc_out_all, j)
        else:
            wts = _mla_weights(mla_w_a[j], mla_g_qa[j], mla_w_uq[j], mla_g_kva[j], mla_w_ukv[j],
                               mla_g_q[j], mla_g_k[j])
            q_c, k_c, v_c = _mla_proj(h_c, mod, i, False, g_all, wts, cos_c, sin_c, n_ctx)
            q_x, k_x, v_x = _mla_proj(h_x, mod, i, True, g_all, wts, cos_x, sin_x, PROJ_TILE)
            pending_outproj = (_attention(q_x, [k_c, k_x], [v_c, v_x]), w_o_all, j)
            if run_ctx_out:
                o_c = _attention(q_c, [k_c], [v_c])
                h_c = _outproj(h_c, o_c, mod, i, False, w_o_all, j, n_ctx)

        h_x = ffn(h_x, i, 2, True, pending_outproj)
        if run_ctx_out:
            h_c = ffn_ctx(h_c, i, 2)
    return h_x
```

```python
import functools
import math

import jax
import jax.numpy as jnp
import numpy as np
from jax.experimental import pallas as pl
from jax.experimental.pallas import tpu as pltpu

GRID_W = 64
N_MOD = 9
MLA_HEADS = 8
QK_NOPE = 128
QK_ROPE = 64
QK_HEAD = QK_NOPE + QK_ROPE
V_HEAD = 128
Q_LORA = 256
KV_LORA = 128
ROPE_BASE = 10000.0
QK_SCALE = QK_HEAD ** -0.5
Q_FOLD = QK_SCALE * math.log2(math.e)
EPS = 1e-6

LANES = 128
SUBLANES = 8
QK_PAD = 2 * LANES
VMEM_LIMIT = 56 * 1024 * 1024
MOD_ROWS = 24

FFN_TILE = 1024
SUB = 512
PROJ_TILE = 512
ATTN_CHUNK = 512
ATTN_STEP_QUERIES = 2048

MAX_BOUND_SHIFT = 48.0
BOUND_MARGIN = 1.02

BF16 = jnp.bfloat16
F32 = jnp.float32

_CONTRACT_LAST = (((1,), (1,)), ((), ()))
_CONTRACT_FIRST = (((0,), (0,)), ((), ()))


def _dot(a, b):
    return jnp.dot(a, b, preferred_element_type=F32)


def _dot_t(a_t, b):
    return jax.lax.dot_general(a_t, b, _CONTRACT_FIRST, preferred_element_type=F32)


def _resident(shape):
    nd = len(shape)
    return pl.BlockSpec(shape, lambda *_: (0,) * nd, pipeline_mode=pl.Buffered(1))


def _stacked(shape, lead):
    nl = len(lead)
    tail = len(shape) - nl
    return pl.BlockSpec((1,) * nl + tuple(shape[nl:]), lambda *_: tuple(lead) + (0,) * tail,
                        pipeline_mode=pl.Buffered(1))


def _params(n_grid):
    return pltpu.CompilerParams(
        dimension_semantics=("arbitrary",) * n_grid, vmem_limit_bytes=VMEM_LIMIT)


def _unit_rows(x):
    n = x.shape[-1]
    return x * jax.lax.rsqrt(jnp.sum(x * x, axis=-1, keepdims=True) + n * EPS)


def _norm_gain(gain):
    return math.sqrt(gain.shape[-1]) * gain


def _pre(h, gain_scale, shift):
    return _unit_rows(h) * gain_scale + shift


def _silu(a):
    return a * jax.nn.sigmoid(a)


def _mod_kernel(cond_ref, w_ref, b_ref, o_ref):
    cond = _silu(cond_ref[...]).astype(BF16)
    o_ref[0] = _dot(cond, w_ref[0].astype(BF16)) + b_ref[0]


def _modulation(cond, w_mod, b_mod):
    depth, d, nm = w_mod.shape
    tn = nm // 4
    return pl.pallas_call(
        _mod_kernel,
        grid=(depth, nm // tn),
        in_specs=[
            pl.BlockSpec((MOD_ROWS, d), lambda i, j: (0, 0)),
            pl.BlockSpec((1, d, tn), lambda i, j: (i, 0, j)),
            pl.BlockSpec((1, 1, tn), lambda i, j: (i, 0, j)),
        ],
        out_specs=pl.BlockSpec((1, MOD_ROWS, tn), lambda i, j: (i, 0, j)),
        out_shape=jax.ShapeDtypeStruct((depth, MOD_ROWS, nm), F32),
        compiler_params=_params(2),
        name="adaln_mod",
    )(cond, w_mod, b_mod.reshape(depth, 1, nm))


def _mod_spec(layer, sub, per_sample, d):
    if per_sample:
        return pl.BlockSpec((1, 1, 1, 3, d), lambda b, t: (layer, b, sub, 0, 0))
    ctx_row = MOD_ROWS - SUBLANES
    return pl.BlockSpec((1, 1, 1, 3, d), lambda b, t: (layer, ctx_row, sub, 0, 0))


def _gain_spec(layer, sub, d):
    return pl.BlockSpec((1, 1, d), lambda b, t: (3 * layer + sub, 0, 0))


def _ffn_kernel(*refs, fused_outproj):
    if fused_outproj:
        h_ref, ot_ref, mix_ref, wo_ref, mod_ref, g_ref, w1_ref, w3_ref, w2_ref, o_ref = refs
    else:
        h_ref, mod_ref, g_ref, w1_ref, w3_ref, w2_ref, o_ref = refs
    mod = mod_ref[0, 0, 0]
    gain_scale = _norm_gain(g_ref[0]) * (1.0 + mod[1:2])
    shift = mod[0:1]
    half_gate = 0.5 * mod[2:3]
    tm = h_ref.shape[1]
    sub = min(SUB, tm)
    for s in range(tm // sub):
        rows = slice(s * sub, (s + 1) * sub)
        h = h_ref[0, rows, :]
        if fused_outproj:
            h = h + mix_ref[0, 0, 0][2:3] * _dot_t(ot_ref[0, :, rows], wo_ref[0])
        xn = _pre(h, gain_scale, shift).astype(BF16)
        a = _dot(xn, w1_ref[0, 0])
        b = _dot(xn, w3_ref[0, 0])
        gated = (_silu(a) * b).astype(BF16)
        o_ref[0, rows, :] = h + half_gate * _dot(gated, w2_ref[0, 0])


def _ffn(h, mod, layer, sub, per_sample, g_all, w1_all, w3_all, w2_all, tm, outproj=None):
    bsz, n, d = h.shape
    tile = pl.BlockSpec((1, tm, d), lambda b, t: (b, t, 0))
    which = sub // 2
    specs = [_mod_spec(layer, sub, per_sample, d), _gain_spec(layer, sub, d),
             _stacked(w1_all.shape, (layer, which)), _stacked(w3_all.shape, (layer, which)),
             _stacked(w2_all.shape, (layer, which))]
    args = [mod, g_all, w1_all, w3_all, w2_all]
    if outproj is not None:
        o_t, w_o_all, j = outproj
        specs = [pl.BlockSpec((1, o_t.shape[1], tm), lambda b, t: (b, 0, t)),
                 _mod_spec(layer, 1, per_sample, d), _stacked(w_o_all.shape, (j,))] + specs
        args = [o_t, mod, w_o_all] + args
    return pl.pallas_call(
        functools.partial(_ffn_kernel, fused_outproj=outproj is not None),
        grid=(bsz, n // tm),
        in_specs=[tile] + specs,
        out_specs=tile,
        out_shape=jax.ShapeDtypeStruct(h.shape, F32),
        compiler_params=_params(2),
        name="ffn",
    )(h, *args)


def _shortconv_kernel(h_ref, mod_ref, g_ref, win_ref, cw_ref, wout_ref, o_ref, cu_scr):
    n, d = h_ref.shape[1], h_ref.shape[2]
    sub = min(SUB, n)
    n_sub = n // sub
    mod = mod_ref[0, 0, 0]
    gain_scale = _norm_gain(g_ref[0]) * (1.0 + mod[1:2])
    shift = mod[0:1]
    gate = mod[2:3]
    cw = cw_ref[0]
    zero_row = jnp.zeros((1, d), F32)
    cu_scr[SUBLANES - 1:SUBLANES, :] = zero_row
    cu_scr[n + SUBLANES:n + SUBLANES + 1, :] = zero_row

    def project(s):
        rows = slice(s * sub, (s + 1) * sub)
        xn = _pre(h_ref[0, rows, :], gain_scale, shift).astype(BF16)
        p = _dot(xn, win_ref[0])
        cu_scr[SUBLANES + s * sub:SUBLANES + (s + 1) * sub, :] = p[:, d:2 * d] * p[:, 2 * d:]
        return p[:, :d]

    def mix(s, b_gate):
        rows = slice(s * sub, (s + 1) * sub)
        lo = SUBLANES + s * sub
        conv = (cw[0:1] * cu_scr[lo - 1:lo - 1 + sub, :] + cw[1:2] * cu_scr[lo:lo + sub, :]
                + cw[2:3] * cu_scr[lo + 1:lo + 1 + sub, :])
        y = (b_gate * conv).astype(BF16)
        o_ref[0, rows, :] = h_ref[0, rows, :] + gate * _dot(y, wout_ref[0])

    b_gate = project(0)
    for s in range(n_sub):
        b_next = project(s + 1) if s + 1 < n_sub else None
        mix(s, b_gate)
        b_gate = b_next


def _shortconv(h, mod, layer, per_sample, g_all, w_in_all, conv_all, w_out_all, j):
    bsz, n, d = h.shape
    tile = pl.BlockSpec((1, n, d), lambda b, t: (b, 0, 0))
    return pl.pallas_call(
        _shortconv_kernel,
        grid=(bsz, 1),
        in_specs=[tile, _mod_spec(layer, 1, per_sample, d), _gain_spec(layer, 1, d),
                  _stacked(w_in_all.shape, (j,)), _stacked(conv_all.shape, (j,)),
                  _stacked(w_out_all.shape, (j,))],
        out_specs=tile,
        out_shape=jax.ShapeDtypeStruct(h.shape, F32),
        scratch_shapes=[pltpu.VMEM((n + 2 * SUBLANES, d), F32)],
        compiler_params=_params(2),
        name="shortconv",
    )(h, mod, g_all, w_in_all, conv_all, w_out_all)


def _mla_proj_kernel(h_ref, mod_ref, g_ref, wa_ref, gqa_ref, wuq_ref, gkva_ref, wukv_ref,
                     gq_ref, gk_ref, cos_ref, sin_ref, q_ref, k_ref, v_ref):
    h = h_ref[0]
    mod = mod_ref[0, 0, 0]
    xn = _pre(h, _norm_gain(g_ref[0]) * (1.0 + mod[1:2]), mod[0:1]).astype(BF16)
    a = _dot(xn, wa_ref[...])
    cq = a[:, :Q_LORA]
    ckv = a[:, Q_LORA:Q_LORA + KV_LORA]
    kr = a[:, Q_LORA + KV_LORA:Q_LORA + KV_LORA + LANES]
    krs = a[:, Q_LORA + KV_LORA + LANES:]

    cqn = (_unit_rows(cq) * _norm_gain(gqa_ref[...])).astype(BF16)
    ckvn = (_unit_rows(ckv) * _norm_gain(gkva_ref[...])).astype(BF16)
    qraw = _dot(cqn, wuq_ref[...])
    kvraw = _dot(ckvn, wukv_ref[...])

    cos = cos_ref[...]
    sin = sin_ref[...]
    gq = gq_ref[...]
    gk = gk_ref[...] * (QK_HEAD * Q_FOLD)
    hw = MLA_HEADS * LANES
    nope_gain = gq[0:1] * gk[0:1]
    kr_rot = kr * (cos * gk[1:2]) + krs * (sin * gk[2:3])
    kr_sq = kr * kr
    q_cos = cos * gq[1:2]
    q_sin = sin * gq[2:3]
    norm_eps = QK_HEAD * EPS
    for hd in range(MLA_HEADS):
        lo = hd * LANES
        qn = qraw[:, lo:lo + LANES]
        qr = qraw[:, hw + lo:hw + lo + LANES]
        qrs = qraw[:, 2 * hw + lo:2 * hw + lo + LANES]
        r = jax.lax.rsqrt(jnp.sum(qn * qn + qr * qr, axis=-1, keepdims=True) + norm_eps)
        q_ref[0, hd, :, :LANES] = (qn * r).astype(BF16)
        q_ref[0, hd, :, LANES:] = ((qr * q_cos + qrs * q_sin) * r).astype(BF16)

        kn = kvraw[:, lo:lo + LANES]
        rk = jax.lax.rsqrt(jnp.sum(kn * kn + kr_sq, axis=-1, keepdims=True) + norm_eps)
        k_ref[0, hd, :, :LANES] = (kn * nope_gain * rk).astype(BF16)
        k_ref[0, hd, :, LANES:] = (kr_rot * rk).astype(BF16)
        v_ref[0, hd] = kvraw[:, hw + lo:hw + lo + LANES].astype(BF16)


def _mla_proj(h, mod, layer, per_sample, g_all, wts, cos, sin, tm):
    bsz, n, d = h.shape
    tile = pl.BlockSpec((1, tm, d), lambda b, t: (b, t, 0))
    tab = pl.BlockSpec((tm, LANES), lambda b, t: (t, 0))
    qk_spec = pl.BlockSpec((1, MLA_HEADS, tm, QK_PAD), lambda b, t: (b, 0, t, 0))
    v_spec = pl.BlockSpec((1, MLA_HEADS, tm, V_HEAD), lambda b, t: (b, 0, t, 0))
    qk_shape = jax.ShapeDtypeStruct((bsz, MLA_HEADS, n, QK_PAD), BF16)
    v_shape = jax.ShapeDtypeStruct((bsz, MLA_HEADS, n, V_HEAD), BF16)
    small = [wts["w_a"], wts["g_qa"], wts["w_uq"], wts["g_kva"], wts["w_ukv"], wts["g_q"], wts["g_k"]]
    return pl.pallas_call(
        _mla_proj_kernel,
        grid=(bsz, n // tm),
        in_specs=[tile, _mod_spec(layer, 1, per_sample, d), _gain_spec(layer, 1, d)]
                 + [_resident(w.shape) for w in small] + [tab, tab],
        out_specs=[qk_spec, qk_spec, v_spec],
        out_shape=[qk_shape, qk_shape, v_shape],
        compiler_params=_params(2),
        name="mla_proj",
    )(h, mod, g_all, *small, cos, sin)


def _attn_kernel(*refs, n_parts):
    bound_ref, q_ref = refs[0], refs[1]
    k_refs = refs[2:2 + n_parts]
    v_refs = refs[2 + n_parts:2 + 2 * n_parts]
    o_ref = refs[2 + 2 * n_parts]
    s_scr = refs[3 + 2 * n_parts]
    n_slots, _, chunk = s_scr.shape
    items = [(hd, c) for hd in range(q_ref.shape[1]) for c in range(q_ref.shape[2] // chunk)]
    sizes = [k.shape[2] for k in k_refs]
    offs = [sum(sizes[:i]) for i in range(n_parts)]

    def q_chunk(hd, c):
        return q_ref[0, hd, c * chunk:(c + 1) * chunk, :]

    def store(hd, c, out, denom):
        o_ref[0, hd * V_HEAD:(hd + 1) * V_HEAD, c * chunk:(c + 1) * chunk] = (
            out * (1.0 / denom)).astype(o_ref.dtype)

    bound = bound_ref[0]
    shift_by_bound = bound <= MAX_BOUND_SHIFT

    @pl.when(shift_by_bound)
    def _():
        for hd, c in items:
            q = q_chunk(hd, c)
            denom = None
            out = None
            for k_ref, v_ref in zip(k_refs, v_refs):
                s = jax.lax.dot_general(k_ref[0, hd], q, _CONTRACT_LAST, preferred_element_type=F32)
                p = jnp.exp2(s - bound)
                part = jnp.sum(p, axis=0, keepdims=True)
                denom = part if denom is None else denom + part
                acc = _dot_t(v_ref[0, hd], p.astype(BF16))
                out = acc if out is None else out + acc
            store(hd, c, out, denom)

    @pl.when(jnp.logical_not(shift_by_bound))
    def _():
        def scores(i):
            hd, c = items[i]
            q = q_chunk(hd, c)
            col_max = None
            for k_ref, off, sz in zip(k_refs, offs, sizes):
                s = jax.lax.dot_general(k_ref[0, hd], q, _CONTRACT_LAST, preferred_element_type=F32)
                s_scr[i % n_slots, off:off + sz, :] = s
                part = jnp.max(s, axis=0, keepdims=True)
                col_max = part if col_max is None else jnp.maximum(col_max, part)
            return col_max

        def attend(i, col_max):
            hd, c = items[i]
            denom = None
            out = None
            for v_ref, off, sz in zip(v_refs, offs, sizes):
                p = jnp.exp2(s_scr[i % n_slots, off:off + sz, :] - col_max)
                part = jnp.sum(p, axis=0, keepdims=True)
                denom = part if denom is None else denom + part
                acc = _dot_t(v_ref[0, hd], p.astype(BF16))
                out = acc if out is None else out + acc
            store(hd, c, out, denom)

        col_max = scores(0)
        for i in range(len(items)):
            next_max = scores(i + 1) if i + 1 < len(items) else None
            attend(i, col_max)
            col_max = next_max


def _score_bound(g_q, g_k):
    q_gain = jnp.maximum(1.0, jnp.max(jnp.abs(g_q[QK_NOPE:])))
    k_gain = jnp.maximum(jnp.max(jnp.abs(g_q[:QK_NOPE] * g_k[:QK_NOPE])), jnp.max(jnp.abs(g_k[QK_NOPE:])))
    return (QK_HEAD * Q_FOLD * BOUND_MARGIN * q_gain * k_gain).reshape(1)


def _attention(bound, q, ks, vs):
    bsz, nh, n, _ = q.shape
    n_parts = len(ks)
    chunk = min(ATTN_CHUNK, n)
    n_keys = sum(k.shape[2] for k in ks)
    heads = max(1, min(nh, ATTN_STEP_QUERIES // n))
    n_items = heads * (n // chunk)
    q_spec = pl.BlockSpec((1, heads, n, QK_PAD), lambda b, h, *_: (b, h, 0, 0))
    k_specs = [pl.BlockSpec((1, heads, k.shape[2], QK_PAD), lambda b, h, *_: (b, h, 0, 0)) for k in ks]
    v_specs = [pl.BlockSpec((1, heads, v.shape[2], V_HEAD), lambda b, h, *_: (b, h, 0, 0)) for v in vs]
    return pl.pallas_call(
        functools.partial(_attn_kernel, n_parts=n_parts),
        grid_spec=pltpu.PrefetchScalarGridSpec(
            num_scalar_prefetch=1,
            grid=(bsz, nh // heads),
            in_specs=[q_spec] + k_specs + v_specs,
            out_specs=pl.BlockSpec((1, heads * V_HEAD, n), lambda b, h, *_: (b, h, 0)),
            scratch_shapes=[pltpu.VMEM((min(2, n_items), n_keys, chunk), F32)],
        ),
        out_shape=jax.ShapeDtypeStruct((bsz, nh * V_HEAD, n), BF16),
        compiler_params=_params(2),
        name="attention",
    )(bound, q, *ks, *vs)


def _outproj_kernel(h_ref, o_ref, mod_ref, wo_ref, out_ref):
    mod = mod_ref[0, 0, 0]
    out_ref[0] = h_ref[0] + mod[2:3] * _dot_t(o_ref[0], wo_ref[0])


def _outproj(h, o, mod, layer, per_sample, w_o_all, j, tm):
    bsz, n, d = h.shape
    tile = pl.BlockSpec((1, tm, d), lambda b, t: (b, t, 0))
    o_tile = pl.BlockSpec((1, o.shape[1], tm), lambda b, t: (b, 0, t))
    return pl.pallas_call(
        _outproj_kernel,
        grid=(bsz, n // tm),
        in_specs=[tile, o_tile, _mod_spec(layer, 1, per_sample, d), _stacked(w_o_all.shape, (j,))],
        out_specs=tile,
        out_shape=jax.ShapeDtypeStruct(h.shape, F32),
        compiler_params=_params(2),
        name="attn_outproj",
    )(h, o, mod, w_o_all)


_SWAP = np.concatenate([np.arange(16, 32), np.arange(0, 16), np.arange(48, 64), np.arange(32, 48)])


def _pad_lanes(w):
    return jnp.concatenate([w, jnp.zeros_like(w)], axis=-1)


def _mla_weights(w_a, g_qa, w_uq, g_kva, w_ukv, g_q, g_k):
    kr = w_a[:, Q_LORA + KV_LORA:]
    w_a2 = jnp.concatenate(
        [w_a[:, :Q_LORA + KV_LORA], _pad_lanes(kr), _pad_lanes(kr[:, _SWAP])], axis=1)
    uq = w_uq.reshape(Q_LORA, MLA_HEADS, QK_HEAD)
    uq_rope = uq[:, :, QK_NOPE:]
    w_uq2 = jnp.concatenate(
        [uq[:, :, :QK_NOPE].reshape(Q_LORA, -1),
         _pad_lanes(uq_rope).reshape(Q_LORA, -1),
         _pad_lanes(uq_rope[:, :, _SWAP]).reshape(Q_LORA, -1)], axis=1)
    ukv = w_ukv.reshape(KV_LORA, MLA_HEADS, QK_NOPE + V_HEAD)
    w_ukv2 = jnp.concatenate(
        [ukv[:, :, :QK_NOPE].reshape(KV_LORA, -1), ukv[:, :, QK_NOPE:].reshape(KV_LORA, -1)], axis=1)

    def gains(gv):
        rope = gv[QK_NOPE:]
        return jnp.stack([gv[:QK_NOPE], _pad_lanes(rope), _pad_lanes(rope[_SWAP])])

    return {
        "w_a": w_a2.astype(BF16), "g_qa": g_qa[None], "w_uq": w_uq2.astype(BF16),
        "g_kva": g_kva[None], "w_ukv": w_ukv2.astype(BF16), "g_q": gains(g_q), "g_k": gains(g_k),
    }


def _rope_tables(n):
    pos = np.arange(n)
    n_freq = QK_ROPE // 4
    inv = ROPE_BASE ** (-np.arange(n_freq, dtype=np.float64) / n_freq)
    ang_r = (pos // GRID_W).astype(np.float64)[:, None] * inv
    ang_c = (pos % GRID_W).astype(np.float64)[:, None] * inv
    zeros = np.zeros((n, LANES - QK_ROPE), np.float64)
    cos = np.concatenate([np.cos(ang_r), np.cos(ang_r), np.cos(ang_c), np.cos(ang_c), zeros], axis=1)
    sin = np.concatenate([-np.sin(ang_r), np.sin(ang_r), -np.sin(ang_c), np.sin(ang_c), zeros], axis=1)
    return jnp.asarray(cos, F32), jnp.asarray(sin, F32)


def _identity_tables(n):
    ones = np.concatenate([np.ones((n, QK_ROPE), np.float32), np.zeros((n, LANES - QK_ROPE), np.float32)], axis=1)
    return jnp.asarray(ones, F32), jnp.zeros((n, LANES), F32)


def kernel(x, c, ctx, c_ctx, w_mod, b_mod, g_norm, ffn_w1, ffn_w3, ffn_w2, sc_w_in, sc_conv, sc_w_out,
           mla_w_a, mla_g_qa, mla_w_uq, mla_g_kva, mla_w_ukv, mla_g_q, mla_g_k, mla_w_o):
    bsz, n, d = x.shape
    n_ctx = ctx.shape[1]
    depth = w_mod.shape[0]

    cond = jnp.concatenate([c, c_ctx[None], jnp.zeros((MOD_ROWS - bsz - 1, d), F32)], axis=0)
    mod = _modulation(cond, w_mod, b_mod).reshape(depth, MOD_ROWS, 3, 3, d)

    g_all = g_norm.reshape(depth * 3, 1, d)
    w1_all, w3_all, w2_all = ffn_w1.astype(BF16), ffn_w3.astype(BF16), ffn_w2.astype(BF16)
    sc_in_all, sc_out_all = sc_w_in.astype(BF16), sc_w_out.astype(BF16)
    w_o_all = mla_w_o.astype(BF16)
    cos_x, sin_x = _rope_tables(n)
    cos_c, sin_c = _identity_tables(n_ctx)

    def ffn(h, layer, sub, per_sample, outproj=None):
        return _ffn(h, mod, layer, sub, per_sample, g_all, w1_all, w3_all, w2_all,
                    min(FFN_TILE, h.shape[1]), outproj)

    def ffn_ctx(h, layer, sub):
        return ffn(h.reshape(1, bsz * n_ctx, d), layer, sub, False).reshape(bsz, n_ctx, d)

    h_x, h_c = x, ctx
    for i in range(depth):
        kind, j = i % 2, i // 2
        last = i == depth - 1
        run_ctx_in = (not last) or kind == 1
        run_ctx_out = not last

        h_x = ffn(h_x, i, 0, True)
        if run_ctx_in:
            h_c = ffn_ctx(h_c, i, 0)

        pending_outproj = None
        if kind == 0:
            h_x = _shortconv(h_x, mod, i, True, g_all, sc_in_all, sc_conv, sc_out_all, j)
            if run_ctx_out:
                h_c = _shortconv(h_c, mod, i, False, g_all, sc_in_all, sc_conv, sc_out_all, j)
        else:
            wts = _mla_weights(mla_w_a[j], mla_g_qa[j], mla_w_uq[j], mla_g_kva[j], mla_w_ukv[j],
                               mla_g_q[j], mla_g_k[j])
            q_c, k_c, v_c = _mla_proj(h_c, mod, i, False, g_all, wts, cos_c, sin_c, n_ctx)
            q_x, k_x, v_x = _mla_proj(h_x, mod, i, True, g_all, wts, cos_x, sin_x, PROJ_TILE)
            bound = _score_bound(mla_g_q[j], mla_g_k[j])
            pending_outproj = (_attention(bound, q_x, [k_c, k_x], [v_c, v_x]), w_o_all, j)
            if run_ctx_out:
                o_c = _attention(bound, q_c, [k_c], [v_c])
                h_c = _outproj(h_c, o_c, mod, i, False, w_o_all, j, n_ctx)

        h_x = ffn(h_x, i, 2, True, pending_outproj)
        if run_ctx_out:
            h_c = ffn_ctx(h_c, i, 2)
    return h_x
```

```python
import functools
import math

import jax
import jax.numpy as jnp
import numpy as np
from jax.experimental import pallas as pl
from jax.experimental.pallas import tpu as pltpu

GRID_W = 64
N_MOD = 9
MLA_HEADS = 8
QK_NOPE = 128
QK_ROPE = 64
QK_HEAD = QK_NOPE + QK_ROPE
V_HEAD = 128
Q_LORA = 256
KV_LORA = 128
ROPE_BASE = 10000.0
QK_SCALE = QK_HEAD ** -0.5
Q_FOLD = QK_SCALE * math.log2(math.e)
EPS = 1e-6

LANES = 128
SUBLANES = 8
QK_PAD = 2 * LANES
VMEM_LIMIT = 56 * 1024 * 1024
MOD_ROWS = 24

FFN_TILE = 1024
FFN_CTX_TILE = 512
FFN_BLOCK = 256
SUB = 512
PROJ_TILE = 512
ATTN_CHUNK = 512
ATTN_STEP_QUERIES = 2048

MAX_BOUND_SHIFT = 48.0
BOUND_MARGIN = 1.02

BF16 = jnp.bfloat16
F32 = jnp.float32

_CONTRACT_LAST = (((1,), (1,)), ((), ()))
_CONTRACT_FIRST = (((0,), (0,)), ((), ()))


def _dot(a, b):
    return jnp.dot(a, b, preferred_element_type=F32)


def _dot_t(a_t, b):
    return jax.lax.dot_general(a_t, b, _CONTRACT_FIRST, preferred_element_type=F32)


def _resident(shape):
    nd = len(shape)
    return pl.BlockSpec(shape, lambda *_: (0,) * nd, pipeline_mode=pl.Buffered(1))


def _stacked(shape, lead):
    nl = len(lead)
    tail = len(shape) - nl
    return pl.BlockSpec((1,) * nl + tuple(shape[nl:]), lambda *_: tuple(lead) + (0,) * tail,
                        pipeline_mode=pl.Buffered(1))


def _params(n_grid):
    return pltpu.CompilerParams(
        dimension_semantics=("arbitrary",) * n_grid, vmem_limit_bytes=VMEM_LIMIT)


def _unit_rows(x):
    n = x.shape[-1]
    return x * jax.lax.rsqrt(jnp.sum(x * x, axis=-1, keepdims=True) + n * EPS)


def _norm_gain(gain):
    return math.sqrt(gain.shape[-1]) * gain


def _pre(h, gain_scale, shift):
    return _unit_rows(h) * gain_scale + shift


def _silu(a):
    return a * jax.nn.sigmoid(a)


def _mod_kernel(cond_ref, w_ref, b_ref, o_ref):
    cond = _silu(cond_ref[...]).astype(BF16)
    o_ref[0] = _dot(cond, w_ref[0].astype(BF16)) + b_ref[0]


def _modulation(cond, w_mod, b_mod):
    depth, d, nm = w_mod.shape
    tn = nm // 4
    return pl.pallas_call(
        _mod_kernel,
        grid=(depth, nm // tn),
        in_specs=[
            pl.BlockSpec((MOD_ROWS, d), lambda i, j: (0, 0)),
            pl.BlockSpec((1, d, tn), lambda i, j: (i, 0, j)),
            pl.BlockSpec((1, 1, tn), lambda i, j: (i, 0, j)),
        ],
        out_specs=pl.BlockSpec((1, MOD_ROWS, tn), lambda i, j: (i, 0, j)),
        out_shape=jax.ShapeDtypeStruct((depth, MOD_ROWS, nm), F32),
        compiler_params=_params(2),
        name="adaln_mod",
    )(cond, w_mod, b_mod.reshape(depth, 1, nm))


CTX_MOD_ROW = MOD_ROWS - SUBLANES


def _mod_spec(layer, sub, per_sample, d, sample_of=lambda b, *_: b):
    if per_sample:
        return pl.BlockSpec((1, 1, 1, 3, d), lambda *idx: (layer, sample_of(*idx), sub, 0, 0))
    return pl.BlockSpec((1, 1, 1, 3, d), lambda *_: (layer, CTX_MOD_ROW, sub, 0, 0))


def _gain_spec(layer, sub, d):
    return pl.BlockSpec((1, 1, d), lambda *_: (3 * layer + sub, 0, 0))


def _ffn_tile(h_ref, mod, g_ref, w13_ref, w2_ref, o_ref, outproj=None):
    gain_scale = _norm_gain(g_ref[0]) * (1.0 + mod[1:2])
    shift = mod[0:1]
    half_gate = 0.5 * mod[2:3]
    tm = h_ref.shape[1]
    sub = min(SUB, tm)
    dff = w2_ref.shape[2]
    for s in range(tm // sub):
        rows = slice(s * sub, (s + 1) * sub)
        h = h_ref[0, rows, :]
        if outproj is not None:
            ot_ref, mix_gate, wo_ref = outproj
            h = h + mix_gate * _dot_t(ot_ref[0, :, rows], wo_ref[0])
        xn = _pre(h, gain_scale, shift).astype(BF16)
        p = _dot(xn, w13_ref[0, 0])
        gated = jnp.concatenate(
            [(_silu(p[:, c:c + FFN_BLOCK]) * p[:, c + FFN_BLOCK:c + 2 * FFN_BLOCK]).astype(BF16)
             for c in range(0, 2 * dff, 2 * FFN_BLOCK)], axis=1)
        o_ref[0, rows, :] = h + half_gate * _dot(gated, w2_ref[0, 0])


def _ffn_kernel(*refs, fused_outproj, with_ctx, n_latent_steps):
    refs = list(refs)
    hx_ref = refs.pop(0)
    outproj_refs = [refs.pop(0) for _ in range(3)] if fused_outproj else None
    modx_ref = refs.pop(0)
    ctx_refs = [refs.pop(0) for _ in range(2)] if with_ctx else None
    g_ref, w13_ref, w2_ref = refs[:3]
    outs = refs[3:]

    def latent():
        outproj = None
        if fused_outproj:
            ot_ref, mix_ref, wo_ref = outproj_refs
            outproj = (ot_ref, mix_ref[0, 0, 0][2:3], wo_ref)
        _ffn_tile(hx_ref, modx_ref[0, 0, 0], g_ref, w13_ref, w2_ref, outs[0], outproj)

    if not with_ctx:
        latent()
        return
    step = pl.program_id(0)
    pl.when(step < n_latent_steps)(latent)

    @pl.when(step >= n_latent_steps)
    def _():
        hc_ref, modc_ref = ctx_refs
        _ffn_tile(hc_ref, modc_ref[0, 0, 0], g_ref, w13_ref, w2_ref, outs[1])


def _ffn(h_x, h_c, mod, layer, sub, g_all, w13_all, w2_all, outproj=None):
    bsz, n, d = h_x.shape
    tm = min(FFN_TILE, n)
    nt = n // tm
    n_latent = bsz * nt

    def sample(i):
        return jnp.minimum(i, n_latent - 1) // nt

    def tile(i):
        return jnp.minimum(i, n_latent - 1) % nt

    which = sub // 2
    specs = [pl.BlockSpec((1, tm, d), lambda i: (sample(i), tile(i), 0))]
    args = [h_x]
    if outproj is not None:
        o_t, w_o_all, j = outproj
        specs += [pl.BlockSpec((1, o_t.shape[1], tm), lambda i: (sample(i), 0, tile(i))),
                  _mod_spec(layer, 1, True, d, sample), _stacked(w_o_all.shape, (j,))]
        args += [o_t, mod, w_o_all]
    specs.append(_mod_spec(layer, sub, True, d, sample))
    args.append(mod)
    out_specs = [specs[0]]
    out_shapes = [jax.ShapeDtypeStruct(h_x.shape, F32)]
    n_steps = n_latent
    if h_c is not None:
        flat = h_c.reshape(1, -1, d)
        tc = min(FFN_CTX_TILE, flat.shape[1])
        ctx_tile = pl.BlockSpec((1, tc, d), lambda i: (0, jnp.maximum(i - n_latent, 0), 0))
        specs += [ctx_tile, _mod_spec(layer, sub, False, d)]
        args += [flat, mod]
        out_specs.append(ctx_tile)
        out_shapes.append(jax.ShapeDtypeStruct(flat.shape, F32))
        n_steps += flat.shape[1] // tc
    specs += [_gain_spec(layer, sub, d), _stacked(w13_all.shape, (layer, which)),
              _stacked(w2_all.shape, (layer, which))]
    args += [g_all, w13_all, w2_all]
    outs = pl.pallas_call(
        functools.partial(_ffn_kernel, fused_outproj=outproj is not None, with_ctx=h_c is not None,
                          n_latent_steps=n_latent),
        grid=(n_steps,),
        in_specs=specs,
        out_specs=out_specs,
        out_shape=out_shapes,
        compiler_params=_params(1),
        name="ffn",
    )(*args)
    return outs[0], (outs[1].reshape(h_c.shape) if h_c is not None else None)


def _shortconv_kernel(h_ref, mod_ref, g_ref, win_ref, cw_ref, wout_ref, o_ref, cu_scr):
    n, d = h_ref.shape[1], h_ref.shape[2]
    sub = min(SUB, n)
    n_sub = n // sub
    mod = mod_ref[0, 0, 0]
    gain_scale = _norm_gain(g_ref[0]) * (1.0 + mod[1:2])
    shift = mod[0:1]
    gate = mod[2:3]
    cw = cw_ref[0]
    zero_row = jnp.zeros((1, d), F32)
    cu_scr[SUBLANES - 1:SUBLANES, :] = zero_row
    cu_scr[n + SUBLANES:n + SUBLANES + 1, :] = zero_row

    def project(s):
        rows = slice(s * sub, (s + 1) * sub)
        xn = _pre(h_ref[0, rows, :], gain_scale, shift).astype(BF16)
        p = _dot(xn, win_ref[0])
        cu_scr[SUBLANES + s * sub:SUBLANES + (s + 1) * sub, :] = p[:, d:2 * d] * p[:, 2 * d:]
        return p[:, :d]

    def mix(s, b_gate):
        rows = slice(s * sub, (s + 1) * sub)
        lo = SUBLANES + s * sub
        conv = (cw[0:1] * cu_scr[lo - 1:lo - 1 + sub, :] + cw[1:2] * cu_scr[lo:lo + sub, :]
                + cw[2:3] * cu_scr[lo + 1:lo + 1 + sub, :])
        y = (b_gate * conv).astype(BF16)
        o_ref[0, rows, :] = h_ref[0, rows, :] + gate * _dot(y, wout_ref[0])

    b_gate = project(0)
    for s in range(n_sub):
        b_next = project(s + 1) if s + 1 < n_sub else None
        mix(s, b_gate)
        b_gate = b_next


def _shortconv(h, mod, layer, per_sample, g_all, w_in_all, conv_all, w_out_all, j):
    bsz, n, d = h.shape
    tile = pl.BlockSpec((1, n, d), lambda b, t: (b, 0, 0))
    return pl.pallas_call(
        _shortconv_kernel,
        grid=(bsz, 1),
        in_specs=[tile, _mod_spec(layer, 1, per_sample, d), _gain_spec(layer, 1, d),
                  _stacked(w_in_all.shape, (j,)), _stacked(conv_all.shape, (j,)),
                  _stacked(w_out_all.shape, (j,))],
        out_specs=tile,
        out_shape=jax.ShapeDtypeStruct(h.shape, F32),
        scratch_shapes=[pltpu.VMEM((n + 2 * SUBLANES, d), F32)],
        compiler_params=_params(2),
        name="shortconv",
    )(h, mod, g_all, w_in_all, conv_all, w_out_all)


def _mla_proj_kernel(h_ref, mod_ref, g_ref, wa_ref, gqa_ref, wuq_ref, gkva_ref, wukv_ref,
                     gq_ref, gk_ref, cos_ref, sin_ref, q_ref, k_ref, v_ref):
    h = h_ref[0]
    mod = mod_ref[0, 0, 0]
    xn = _pre(h, _norm_gain(g_ref[0]) * (1.0 + mod[1:2]), mod[0:1]).astype(BF16)
    a = _dot(xn, wa_ref[...])
    cq = a[:, :Q_LORA]
    ckv = a[:, Q_LORA:Q_LORA + KV_LORA]
    kr = a[:, Q_LORA + KV_LORA:Q_LORA + KV_LORA + LANES]
    krs = a[:, Q_LORA + KV_LORA + LANES:]

    cqn = (_unit_rows(cq) * _norm_gain(gqa_ref[...])).astype(BF16)
    ckvn = (_unit_rows(ckv) * _norm_gain(gkva_ref[...])).astype(BF16)
    qraw = _dot(cqn, wuq_ref[...])
    kvraw = _dot(ckvn, wukv_ref[...])

    cos = cos_ref[...]
    sin = sin_ref[...]
    gq = gq_ref[...]
    gk = gk_ref[...] * (QK_HEAD * Q_FOLD)
    hw = MLA_HEADS * LANES
    nope_gain = gq[0:1] * gk[0:1]
    kr_rot = kr * (cos * gk[1:2]) + krs * (sin * gk[2:3])
    kr_sq = kr * kr
    q_cos = cos * gq[1:2]
    q_sin = sin * gq[2:3]
    norm_eps = QK_HEAD * EPS
    for hd in range(MLA_HEADS):
        lo = hd * LANES
        qn = qraw[:, lo:lo + LANES]
        qr = qraw[:, hw + lo:hw + lo + LANES]
        qrs = qraw[:, 2 * hw + lo:2 * hw + lo + LANES]
        r = jax.lax.rsqrt(jnp.sum(qn * qn + qr * qr, axis=-1, keepdims=True) + norm_eps)
        q_ref[0, hd, :, :LANES] = (qn * r).astype(BF16)
        q_ref[0, hd, :, LANES:] = ((qr * q_cos + qrs * q_sin) * r).astype(BF16)

        kn = kvraw[:, lo:lo + LANES]
        rk = jax.lax.rsqrt(jnp.sum(kn * kn + kr_sq, axis=-1, keepdims=True) + norm_eps)
        k_ref[0, hd, :, :LANES] = (kn * nope_gain * rk).astype(BF16)
        k_ref[0, hd, :, LANES:] = (kr_rot * rk).astype(BF16)
        v_ref[0, hd] = kvraw[:, hw + lo:hw + lo + LANES].astype(BF16)


def _mla_proj(h, mod, layer, per_sample, g_all, wts, cos, sin, tm):
    bsz, n, d = h.shape
    tile = pl.BlockSpec((1, tm, d), lambda b, t: (b, t, 0))
    tab = pl.BlockSpec((tm, LANES), lambda b, t: (t, 0))
    qk_spec = pl.BlockSpec((1, MLA_HEADS, tm, QK_PAD), lambda b, t: (b, 0, t, 0))
    v_spec = pl.BlockSpec((1, MLA_HEADS, tm, V_HEAD), lambda b, t: (b, 0, t, 0))
    qk_shape = jax.ShapeDtypeStruct((bsz, MLA_HEADS, n, QK_PAD), BF16)
    v_shape = jax.ShapeDtypeStruct((bsz, MLA_HEADS, n, V_HEAD), BF16)
    small = [wts["w_a"], wts["g_qa"], wts["w_uq"], wts["g_kva"], wts["w_ukv"], wts["g_q"], wts["g_k"]]
    return pl.pallas_call(
        _mla_proj_kernel,
        grid=(bsz, n // tm),
        in_specs=[tile, _mod_spec(layer, 1, per_sample, d), _gain_spec(layer, 1, d)]
                 + [_resident(w.shape) for w in small] + [tab, tab],
        out_specs=[qk_spec, qk_spec, v_spec],
        out_shape=[qk_shape, qk_shape, v_shape],
        compiler_params=_params(2),
        name="mla_proj",
    )(h, mod, g_all, *small, cos, sin)


def _attn_kernel(*refs, n_parts):
    bound_ref, q_ref = refs[0], refs[1]
    k_refs = refs[2:2 + n_parts]
    v_refs = refs[2 + n_parts:2 + 2 * n_parts]
    o_ref = refs[2 + 2 * n_parts]
    s_scr = refs[3 + 2 * n_parts]
    n_slots, _, chunk = s_scr.shape
    items = [(hd, c) for hd in range(q_ref.shape[1]) for c in range(q_ref.shape[2] // chunk)]
    sizes = [k.shape[2] for k in k_refs]
    offs = [sum(sizes[:i]) for i in range(n_parts)]

    def q_chunk(hd, c):
        return q_ref[0, hd, c * chunk:(c + 1) * chunk, :]

    def store(hd, c, out, denom):
        o_ref[0, hd * V_HEAD:(hd + 1) * V_HEAD, c * chunk:(c + 1) * chunk] = (
            out * (1.0 / denom)).astype(o_ref.dtype)

    bound = bound_ref[0]
    shift_by_bound = bound <= MAX_BOUND_SHIFT

    @pl.when(shift_by_bound)
    def _():
        for hd, c in items:
            q = q_chunk(hd, c)
            denom = None
            out = None
            for k_ref, v_ref in zip(k_refs, v_refs):
                s = jax.lax.dot_general(k_ref[0, hd], q, _CONTRACT_LAST, preferred_element_type=F32)
                p = jnp.exp2(s - bound)
                part = jnp.sum(p, axis=0, keepdims=True)
                denom = part if denom is None else denom + part
                acc = _dot_t(v_ref[0, hd], p.astype(BF16))
                out = acc if out is None else out + acc
            store(hd, c, out, denom)

    @pl.when(jnp.logical_not(shift_by_bound))
    def _():
        def scores(i):
            hd, c = items[i]
            q = q_chunk(hd, c)
            col_max = None
            for k_ref, off, sz in zip(k_refs, offs, sizes):
                s = jax.lax.dot_general(k_ref[0, hd], q, _CONTRACT_LAST, preferred_element_type=F32)
                s_scr[i % n_slots, off:off + sz, :] = s
                part = jnp.max(s, axis=0, keepdims=True)
                col_max = part if col_max is None else jnp.maximum(col_max, part)
            return col_max

        def attend(i, col_max):
            hd, c = items[i]
            denom = None
            out = None
            for v_ref, off, sz in zip(v_refs, offs, sizes):
                p = jnp.exp2(s_scr[i % n_slots, off:off + sz, :] - col_max)
                part = jnp.sum(p, axis=0, keepdims=True)
                denom = part if denom is None else denom + part
                acc = _dot_t(v_ref[0, hd], p.astype(BF16))
                out = acc if out is None else out + acc
            store(hd, c, out, denom)

        col_max = scores(0)
        for i in range(len(items)):
            next_max = scores(i + 1) if i + 1 < len(items) else None
            attend(i, col_max)
            col_max = next_max


def _score_bound(g_q, g_k):
    q_gain = jnp.maximum(1.0, jnp.max(jnp.abs(g_q[QK_NOPE:])))
    k_gain = jnp.maximum(jnp.max(jnp.abs(g_q[:QK_NOPE] * g_k[:QK_NOPE])), jnp.max(jnp.abs(g_k[QK_NOPE:])))
    return (QK_HEAD * Q_FOLD * BOUND_MARGIN * q_gain * k_gain).reshape(1)


def _attention(bound, q, ks, vs):
    bsz, nh, n, _ = q.shape
    n_parts = len(ks)
    chunk = min(ATTN_CHUNK, n)
    n_keys = sum(k.shape[2] for k in ks)
    heads = max(1, min(nh, ATTN_STEP_QUERIES // n))
    n_items = heads * (n // chunk)
    q_spec = pl.BlockSpec((1, heads, n, QK_PAD), lambda b, h, *_: (b, h, 0, 0))
    k_specs = [pl.BlockSpec((1, heads, k.shape[2], QK_PAD), lambda b, h, *_: (b, h, 0, 0)) for k in ks]
    v_specs = [pl.BlockSpec((1, heads, v.shape[2], V_HEAD), lambda b, h, *_: (b, h, 0, 0)) for v in vs]
    return pl.pallas_call(
        functools.partial(_attn_kernel, n_parts=n_parts),
        grid_spec=pltpu.PrefetchScalarGridSpec(
            num_scalar_prefetch=1,
            grid=(bsz, nh // heads),
            in_specs=[q_spec] + k_specs + v_specs,
            out_specs=pl.BlockSpec((1, heads * V_HEAD, n), lambda b, h, *_: (b, h, 0)),
            scratch_shapes=[pltpu.VMEM((min(2, n_items), n_keys, chunk), F32)],
        ),
        out_shape=jax.ShapeDtypeStruct((bsz, nh * V_HEAD, n), BF16),
        compiler_params=_params(2),
        name="attention",
    )(bound, q, *ks, *vs)


def _outproj_kernel(h_ref, o_ref, mod_ref, wo_ref, out_ref):
    mod = mod_ref[0, 0, 0]
    out_ref[0] = h_ref[0] + mod[2:3] * _dot_t(o_ref[0], wo_ref[0])


def _outproj(h, o, mod, layer, per_sample, w_o_all, j, tm):
    bsz, n, d = h.shape
    tile = pl.BlockSpec((1, tm, d), lambda b, t: (b, t, 0))
    o_tile = pl.BlockSpec((1, o.shape[1], tm), lambda b, t: (b, 0, t))
    return pl.pallas_call(
        _outproj_kernel,
        grid=(bsz, n // tm),
        in_specs=[tile, o_tile, _mod_spec(layer, 1, per_sample, d), _stacked(w_o_all.shape, (j,))],
        out_specs=tile,
        out_shape=jax.ShapeDtypeStruct(h.shape, F32),
        compiler_params=_params(2),
        name="attn_outproj",
    )(h, o, mod, w_o_all)


_SWAP = np.concatenate([np.arange(16, 32), np.arange(0, 16), np.arange(48, 64), np.arange(32, 48)])


def _pad_lanes(w):
    return jnp.concatenate([w, jnp.zeros_like(w)], axis=-1)


def _mla_weights(w_a, g_qa, w_uq, g_kva, w_ukv, g_q, g_k):
    kr = w_a[:, Q_LORA + KV_LORA:]
    w_a2 = jnp.concatenate(
        [w_a[:, :Q_LORA + KV_LORA], _pad_lanes(kr), _pad_lanes(kr[:, _SWAP])], axis=1)
    uq = w_uq.reshape(Q_LORA, MLA_HEADS, QK_HEAD)
    uq_rope = uq[:, :, QK_NOPE:]
    w_uq2 = jnp.concatenate(
        [uq[:, :, :QK_NOPE].reshape(Q_LORA, -1),
         _pad_lanes(uq_rope).reshape(Q_LORA, -1),
         _pad_lanes(uq_rope[:, :, _SWAP]).reshape(Q_LORA, -1)], axis=1)
    ukv = w_ukv.reshape(KV_LORA, MLA_HEADS, QK_NOPE + V_HEAD)
    w_ukv2 = jnp.concatenate(
        [ukv[:, :, :QK_NOPE].reshape(KV_LORA, -1), ukv[:, :, QK_NOPE:].reshape(KV_LORA, -1)], axis=1)

    def gains(gv):
        rope = gv[QK_NOPE:]
        return jnp.stack([gv[:QK_NOPE], _pad_lanes(rope), _pad_lanes(rope[_SWAP])])

    return {
        "w_a": w_a2.astype(BF16), "g_qa": g_qa[None], "w_uq": w_uq2.astype(BF16),
        "g_kva": g_kva[None], "w_ukv": w_ukv2.astype(BF16), "g_q": gains(g_q), "g_k": gains(g_k),
    }


def _interleave_blocks(w1, w3):
    lead, dff = w1.shape[:-1], w1.shape[-1]
    blocks = (*lead, dff // FFN_BLOCK, FFN_BLOCK)
    return jnp.stack([w1.reshape(blocks), w3.reshape(blocks)], axis=-2).reshape(*lead, 2 * dff)


def _rope_tables(n):
    pos = np.arange(n)
    n_freq = QK_ROPE // 4
    inv = ROPE_BASE ** (-np.arange(n_freq, dtype=np.float64) / n_freq)
    ang_r = (pos // GRID_W).astype(np.float64)[:, None] * inv
    ang_c = (pos % GRID_W).astype(np.float64)[:, None] * inv
    zeros = np.zeros((n, LANES - QK_ROPE), np.float64)
    cos = np.concatenate([np.cos(ang_r), np.cos(ang_r), np.cos(ang_c), np.cos(ang_c), zeros], axis=1)
    sin = np.concatenate([-np.sin(ang_r), np.sin(ang_r), -np.sin(ang_c), np.sin(ang_c), zeros], axis=1)
    return jnp.asarray(cos, F32), jnp.asarray(sin, F32)


def _identity_tables(n):
    ones = np.concatenate([np.ones((n, QK_ROPE), np.float32), np.zeros((n, LANES - QK_ROPE), np.float32)], axis=1)
    return jnp.asarray(ones, F32), jnp.zeros((n, LANES), F32)


def kernel(x, c, ctx, c_ctx, w_mod, b_mod, g_norm, ffn_w1, ffn_w3, ffn_w2, sc_w_in, sc_conv, sc_w_out,
           mla_w_a, mla_g_qa, mla_w_uq, mla_g_kva, mla_w_ukv, mla_g_q, mla_g_k, mla_w_o):
    bsz, n, d = x.shape
    n_ctx = ctx.shape[1]
    depth = w_mod.shape[0]

    cond = jnp.concatenate([c, c_ctx[None], jnp.zeros((MOD_ROWS - bsz - 1, d), F32)], axis=0)
    mod = _modulation(cond, w_mod, b_mod).reshape(depth, MOD_ROWS, 3, 3, d)

    g_all = g_norm.reshape(depth * 3, 1, d)
    w13_all = _interleave_blocks(ffn_w1, ffn_w3).astype(BF16)
    w2_all = ffn_w2.astype(BF16)
    sc_in_all, sc_out_all = sc_w_in.astype(BF16), sc_w_out.astype(BF16)
    w_o_all = mla_w_o.astype(BF16)
    cos_x, sin_x = _rope_tables(n)
    cos_c, sin_c = _identity_tables(n_ctx)

    def ffn(h_lat, h_ctx, layer, sub, outproj=None):
        return _ffn(h_lat, h_ctx, mod, layer, sub, g_all, w13_all, w2_all, outproj)

    h_x, h_c = x, ctx
    for i in range(depth):
        kind, j = i % 2, i // 2
        last = i == depth - 1
        run_ctx_in = (not last) or kind == 1
        run_ctx_out = not last

        if run_ctx_in:
            h_x, h_c = ffn(h_x, h_c, i, 0)
        else:
            h_x, _ = ffn(h_x, None, i, 0)

        pending_outproj = None
        if kind == 0:
            h_x = _shortconv(h_x, mod, i, True, g_all, sc_in_all, sc_conv, sc_out_all, j)
            if run_ctx_out:
                h_c = _shortconv(h_c, mod, i, False, g_all, sc_in_all, sc_conv, sc_out_all, j)
        else:
            wts = _mla_weights(mla_w_a[j], mla_g_qa[j], mla_w_uq[j], mla_g_kva[j], mla_w_ukv[j],
                               mla_g_q[j], mla_g_k[j])
            q_c, k_c, v_c = _mla_proj(h_c, mod, i, False, g_all, wts, cos_c, sin_c, n_ctx)
            q_x, k_x, v_x = _mla_proj(h_x, mod, i, True, g_all, wts, cos_x, sin_x, PROJ_TILE)
            bound = _score_bound(mla_g_q[j], mla_g_k[j])
            pending_outproj = (_attention(bound, q_x, [k_c, k_x], [v_c, v_x]), w_o_all, j)
            if run_ctx_out:
                o_c = _attention(bound, q_c, [k_c], [v_c])
                h_c = _outproj(h_c, o_c, mod, i, False, w_o_all, j, n_ctx)

        if run_ctx_out:
            h_x, h_c = ffn(h_x, h_c, i, 2, pending_outproj)
        else:
            h_x, _ = ffn(h_x, None, i, 2, pending_outproj)
    return h_x
```

```python
import functools
import math

import jax
import jax.numpy as jnp
import numpy as np
from jax.experimental import pallas as pl
from jax.experimental.pallas import tpu as pltpu

GRID_W = 64
N_MOD = 9
MLA_HEADS = 8
QK_NOPE = 128
QK_ROPE = 64
QK_HEAD = QK_NOPE + QK_ROPE
V_HEAD = 128
Q_LORA = 256
KV_LORA = 128
ROPE_BASE = 10000.0
QK_SCALE = QK_HEAD ** -0.5
Q_FOLD = QK_SCALE * math.log2(math.e)
EPS = 1e-6

LANES = 128
SUBLANES = 8
QK_PAD = 2 * LANES
VMEM_LIMIT = 56 * 1024 * 1024
MOD_ROWS = 24

FFN_TILE = 1024
FFN_CTX_TILE = 512
SUB = 512
PROJ_TILE = 512
ATTN_CHUNK = 512
ATTN_STEP_QUERIES = 2048

MAX_BOUND_SHIFT = 48.0
BOUND_MARGIN = 1.02

BF16 = jnp.bfloat16
F32 = jnp.float32

_CONTRACT_LAST = (((1,), (1,)), ((), ()))
_CONTRACT_FIRST = (((0,), (0,)), ((), ()))


def _dot(a, b):
    return jnp.dot(a, b, preferred_element_type=F32)


def _dot_t(a_t, b):
    return jax.lax.dot_general(a_t, b, _CONTRACT_FIRST, preferred_element_type=F32)


def _resident(shape):
    nd = len(shape)
    return pl.BlockSpec(shape, lambda *_: (0,) * nd, pipeline_mode=pl.Buffered(1))


def _stacked(shape, lead):
    nl = len(lead)
    tail = len(shape) - nl
    return pl.BlockSpec((1,) * nl + tuple(shape[nl:]), lambda *_: tuple(lead) + (0,) * tail,
                        pipeline_mode=pl.Buffered(1))


def _params(n_grid):
    return pltpu.CompilerParams(
        dimension_semantics=("arbitrary",) * n_grid, vmem_limit_bytes=VMEM_LIMIT)


def _unit_rows(x):
    n = x.shape[-1]
    return x * jax.lax.rsqrt(jnp.sum(x * x, axis=-1, keepdims=True) + n * EPS)


def _norm_gain(gain):
    return math.sqrt(gain.shape[-1]) * gain


def _pre(h, gain_scale, shift):
    return _unit_rows(h) * gain_scale + shift


def _silu(a):
    return a * jax.nn.sigmoid(a)


def _mod_kernel(cond_ref, w_ref, b_ref, o_ref):
    cond = _silu(cond_ref[...]).astype(BF16)
    o_ref[0] = _dot(cond, w_ref[0].astype(BF16)) + b_ref[0]


def _modulation(cond, w_mod, b_mod):
    depth, d, nm = w_mod.shape
    tn = nm // 4
    return pl.pallas_call(
        _mod_kernel,
        grid=(depth, nm // tn),
        in_specs=[
            pl.BlockSpec((MOD_ROWS, d), lambda i, j: (0, 0)),
            pl.BlockSpec((1, d, tn), lambda i, j: (i, 0, j)),
            pl.BlockSpec((1, 1, tn), lambda i, j: (i, 0, j)),
        ],
        out_specs=pl.BlockSpec((1, MOD_ROWS, tn), lambda i, j: (i, 0, j)),
        out_shape=jax.ShapeDtypeStruct((depth, MOD_ROWS, nm), F32),
        compiler_params=_params(2),
        name="adaln_mod",
    )(cond, w_mod, b_mod.reshape(depth, 1, nm))


CTX_MOD_ROW = MOD_ROWS - SUBLANES


def _mod_spec(layer, sub, per_sample, d, sample_of=lambda b, *_: b):
    if per_sample:
        return pl.BlockSpec((1, 1, 1, 3, d), lambda *idx: (layer, sample_of(*idx), sub, 0, 0))
    return pl.BlockSpec((1, 1, 1, 3, d), lambda *_: (layer, CTX_MOD_ROW, sub, 0, 0))


def _gain_spec(layer, sub, d):
    return pl.BlockSpec((1, 1, d), lambda *_: (3 * layer + sub, 0, 0))


def _ffn_tile(h_ref, mod, g_ref, w_refs, o_ref, outproj=None):
    w1_ref, w3_ref, w2_ref = w_refs
    gain_scale = _norm_gain(g_ref[0]) * (1.0 + mod[1:2])
    shift = mod[0:1]
    half_gate = 0.5 * mod[2:3]
    tm = h_ref.shape[1]
    sub = min(SUB, tm)
    for s in range(tm // sub):
        rows = slice(s * sub, (s + 1) * sub)
        h = h_ref[0, rows, :]
        if outproj is not None:
            ot_ref, mix_gate, wo_ref = outproj
            h = h + mix_gate * _dot_t(ot_ref[0, :, rows], wo_ref[0])
        xn = _pre(h, gain_scale, shift).astype(BF16)
        gated = (_silu(_dot(xn, w1_ref[0, 0])) * _dot(xn, w3_ref[0, 0])).astype(BF16)
        o_ref[0, rows, :] = h + half_gate * _dot(gated, w2_ref[0, 0])


def _ffn_kernel(*refs, fused_outproj, with_ctx, n_latent_steps):
    refs = list(refs)
    hx_ref = refs.pop(0)
    outproj_refs = [refs.pop(0) for _ in range(3)] if fused_outproj else None
    modx_ref = refs.pop(0)
    ctx_refs = [refs.pop(0) for _ in range(2)] if with_ctx else None
    g_ref = refs[0]
    w_refs = refs[1:4]
    outs = refs[4:]

    def latent():
        outproj = None
        if fused_outproj:
            ot_ref, mix_ref, wo_ref = outproj_refs
            outproj = (ot_ref, mix_ref[0, 0, 0][2:3], wo_ref)
        _ffn_tile(hx_ref, modx_ref[0, 0, 0], g_ref, w_refs, outs[0], outproj)

    if not with_ctx:
        latent()
        return
    step = pl.program_id(0)
    pl.when(step < n_latent_steps)(latent)

    @pl.when(step >= n_latent_steps)
    def _():
        hc_ref, modc_ref = ctx_refs
        _ffn_tile(hc_ref, modc_ref[0, 0, 0], g_ref, w_refs, outs[1])


def _ffn(h_x, h_c, mod, layer, sub, g_all, w1_all, w3_all, w2_all, outproj=None):
    bsz, n, d = h_x.shape
    tm = min(FFN_TILE, n)
    nt = n // tm
    n_latent = bsz * nt

    def sample(i):
        return jnp.minimum(i, n_latent - 1) // nt

    def tile(i):
        return jnp.minimum(i, n_latent - 1) % nt

    which = sub // 2
    specs = [pl.BlockSpec((1, tm, d), lambda i: (sample(i), tile(i), 0))]
    args = [h_x]
    if outproj is not None:
        o_t, w_o_all, j = outproj
        specs += [pl.BlockSpec((1, o_t.shape[1], tm), lambda i: (sample(i), 0, tile(i))),
                  _mod_spec(layer, 1, True, d, sample), _stacked(w_o_all.shape, (j,))]
        args += [o_t, mod, w_o_all]
    specs.append(_mod_spec(layer, sub, True, d, sample))
    args.append(mod)
    out_specs = [specs[0]]
    out_shapes = [jax.ShapeDtypeStruct(h_x.shape, F32)]
    n_steps = n_latent
    if h_c is not None:
        flat = h_c.reshape(1, -1, d)
        tc = min(FFN_CTX_TILE, flat.shape[1])
        ctx_tile = pl.BlockSpec((1, tc, d), lambda i: (0, jnp.maximum(i - n_latent, 0), 0))
        specs += [ctx_tile, _mod_spec(layer, sub, False, d)]
        args += [flat, mod]
        out_specs.append(ctx_tile)
        out_shapes.append(jax.ShapeDtypeStruct(flat.shape, F32))
        n_steps += flat.shape[1] // tc
    specs += [_gain_spec(layer, sub, d)] + [_stacked(w.shape, (layer, which)) for w in (w1_all, w3_all, w2_all)]
    args += [g_all, w1_all, w3_all, w2_all]
    outs = pl.pallas_call(
        functools.partial(_ffn_kernel, fused_outproj=outproj is not None, with_ctx=h_c is not None,
                          n_latent_steps=n_latent),
        grid=(n_steps,),
        in_specs=specs,
        out_specs=out_specs,
        out_shape=out_shapes,
        compiler_params=_params(1),
        name="ffn",
    )(*args)
    return outs[0], (outs[1].reshape(h_c.shape) if h_c is not None else None)


def _shortconv_kernel(h_ref, mod_ref, g_ref, win_ref, cw_ref, wout_ref, o_ref, cu_scr):
    n, d = h_ref.shape[1], h_ref.shape[2]
    sub = min(SUB, n)
    n_sub = n // sub
    mod = mod_ref[0, 0, 0]
    gain_scale = _norm_gain(g_ref[0]) * (1.0 + mod[1:2])
    shift = mod[0:1]
    gate = mod[2:3]
    cw = cw_ref[0]
    zero_row = jnp.zeros((1, d), F32)
    cu_scr[SUBLANES - 1:SUBLANES, :] = zero_row
    cu_scr[n + SUBLANES:n + SUBLANES + 1, :] = zero_row

    def project(s):
        rows = slice(s * sub, (s + 1) * sub)
        xn = _pre(h_ref[0, rows, :], gain_scale, shift).astype(BF16)
        p = _dot(xn, win_ref[0])
        cu_scr[SUBLANES + s * sub:SUBLANES + (s + 1) * sub, :] = p[:, d:2 * d] * p[:, 2 * d:]
        return p[:, :d]

    def mix(s, b_gate):
        rows = slice(s * sub, (s + 1) * sub)
        lo = SUBLANES + s * sub
        conv = (cw[0:1] * cu_scr[lo - 1:lo - 1 + sub, :] + cw[1:2] * cu_scr[lo:lo + sub, :]
                + cw[2:3] * cu_scr[lo + 1:lo + 1 + sub, :])
        y = (b_gate * conv).astype(BF16)
        o_ref[0, rows, :] = h_ref[0, rows, :] + gate * _dot(y, wout_ref[0])

    b_gate = project(0)
    for s in range(n_sub):
        b_next = project(s + 1) if s + 1 < n_sub else None
        mix(s, b_gate)
        b_gate = b_next


def _shortconv(h, mod, layer, per_sample, g_all, w_in_all, conv_all, w_out_all, j):
    bsz, n, d = h.shape
    tile = pl.BlockSpec((1, n, d), lambda b, t: (b, 0, 0))
    return pl.pallas_call(
        _shortconv_kernel,
        grid=(bsz, 1),
        in_specs=[tile, _mod_spec(layer, 1, per_sample, d), _gain_spec(layer, 1, d),
                  _stacked(w_in_all.shape, (j,)), _stacked(conv_all.shape, (j,)),
                  _stacked(w_out_all.shape, (j,))],
        out_specs=tile,
        out_shape=jax.ShapeDtypeStruct(h.shape, F32),
        scratch_shapes=[pltpu.VMEM((n + 2 * SUBLANES, d), F32)],
        compiler_params=_params(2),
        name="shortconv",
    )(h, mod, g_all, w_in_all, conv_all, w_out_all)


def _mla_proj_kernel(h_ref, mod_ref, g_ref, wa_ref, gqa_ref, wuq_ref, gkva_ref, wukv_ref,
                     gq_ref, gk_ref, cos_ref, sin_ref, q_ref, k_ref, v_ref):
    h = h_ref[0]
    mod = mod_ref[0, 0, 0]
    xn = _pre(h, _norm_gain(g_ref[0]) * (1.0 + mod[1:2]), mod[0:1]).astype(BF16)
    a = _dot(xn, wa_ref[...])
    cq = a[:, :Q_LORA]
    ckv = a[:, Q_LORA:Q_LORA + KV_LORA]
    kr = a[:, Q_LORA + KV_LORA:Q_LORA + KV_LORA + LANES]
    krs = a[:, Q_LORA + KV_LORA + LANES:]

    cqn = (_unit_rows(cq) * _norm_gain(gqa_ref[...])).astype(BF16)
    ckvn = (_unit_rows(ckv) * _norm_gain(gkva_ref[...])).astype(BF16)
    qraw = _dot(cqn, wuq_ref[...])
    kvraw = _dot(ckvn, wukv_ref[...])

    cos = cos_ref[...]
    sin = sin_ref[...]
    gq = gq_ref[...]
    gk = gk_ref[...] * (QK_HEAD * Q_FOLD)
    hw = MLA_HEADS * LANES
    nope_gain = gq[0:1] * gk[0:1]
    kr_rot = kr * (cos * gk[1:2]) + krs * (sin * gk[2:3])
    kr_sq = kr * kr
    q_cos = cos * gq[1:2]
    q_sin = sin * gq[2:3]
    norm_eps = QK_HEAD * EPS
    for hd in range(MLA_HEADS):
        lo = hd * LANES
        qn = qraw[:, lo:lo + LANES]
        qr = qraw[:, hw + lo:hw + lo + LANES]
        qrs = qraw[:, 2 * hw + lo:2 * hw + lo + LANES]
        r = jax.lax.rsqrt(jnp.sum(qn * qn + qr * qr, axis=-1, keepdims=True) + norm_eps)
        q_ref[0, hd, :, :LANES] = (qn * r).astype(BF16)
        q_ref[0, hd, :, LANES:] = ((qr * q_cos + qrs * q_sin) * r).astype(BF16)

        kn = kvraw[:, lo:lo + LANES]
        rk = jax.lax.rsqrt(jnp.sum(kn * kn + kr_sq, axis=-1, keepdims=True) + norm_eps)
        k_ref[0, hd, :, :LANES] = (kn * nope_gain * rk).astype(BF16)
        k_ref[0, hd, :, LANES:] = (kr_rot * rk).astype(BF16)
        v_ref[0, hd] = kvraw[:, hw + lo:hw + lo + LANES].astype(BF16)


def _mla_proj(h, mod, layer, per_sample, g_all, wts, cos, sin, tm):
    bsz, n, d = h.shape
    tile = pl.BlockSpec((1, tm, d), lambda b, t: (b, t, 0))
    tab = pl.BlockSpec((tm, LANES), lambda b, t: (t, 0))
    qk_spec = pl.BlockSpec((1, MLA_HEADS, tm, QK_PAD), lambda b, t: (b, 0, t, 0))
    v_spec = pl.BlockSpec((1, MLA_HEADS, tm, V_HEAD), lambda b, t: (b, 0, t, 0))
    qk_shape = jax.ShapeDtypeStruct((bsz, MLA_HEADS, n, QK_PAD), BF16)
    v_shape = jax.ShapeDtypeStruct((bsz, MLA_HEADS, n, V_HEAD), BF16)
    small = [wts["w_a"], wts["g_qa"], wts["w_uq"], wts["g_kva"], wts["w_ukv"], wts["g_q"], wts["g_k"]]
    return pl.pallas_call(
        _mla_proj_kernel,
        grid=(bsz, n // tm),
        in_specs=[tile, _mod_spec(layer, 1, per_sample, d), _gain_spec(layer, 1, d)]
                 + [_resident(w.shape) for w in small] + [tab, tab],
        out_specs=[qk_spec, qk_spec, v_spec],
        out_shape=[qk_shape, qk_shape, v_shape],
        compiler_params=_params(2),
        name="mla_proj",
    )(h, mod, g_all, *small, cos, sin)


def _attn_kernel(*refs, n_parts):
    bound_ref, q_ref = refs[0], refs[1]
    k_refs = refs[2:2 + n_parts]
    v_refs = refs[2 + n_parts:2 + 2 * n_parts]
    o_ref = refs[2 + 2 * n_parts]
    s_scr = refs[3 + 2 * n_parts]
    n_slots, _, chunk = s_scr.shape
    items = [(hd, c) for hd in range(q_ref.shape[1]) for c in range(q_ref.shape[2] // chunk)]
    sizes = [k.shape[2] for k in k_refs]
    offs = [sum(sizes[:i]) for i in range(n_parts)]

    def q_chunk(hd, c):
        return q_ref[0, hd, c * chunk:(c + 1) * chunk, :]

    def store(hd, c, out, denom):
        o_ref[0, hd * V_HEAD:(hd + 1) * V_HEAD, c * chunk:(c + 1) * chunk] = (
            out * (1.0 / denom)).astype(o_ref.dtype)

    bound = bound_ref[0]
    shift_by_bound = bound <= MAX_BOUND_SHIFT

    @pl.when(shift_by_bound)
    def _():
        for hd, c in items:
            q = q_chunk(hd, c)
            denom = None
            out = None
            for k_ref, v_ref in zip(k_refs, v_refs):
                s = jax.lax.dot_general(k_ref[0, hd], q, _CONTRACT_LAST, preferred_element_type=F32)
                p = jnp.exp2(s - bound)
                part = jnp.sum(p, axis=0, keepdims=True)
                denom = part if denom is None else denom + part
                acc = _dot_t(v_ref[0, hd], p.astype(BF16))
                out = acc if out is None else out + acc
            store(hd, c, out, denom)

    @pl.when(jnp.logical_not(shift_by_bound))
    def _():
        def scores(i):
            hd, c = items[i]
            q = q_chunk(hd, c)
            col_max = None
            for k_ref, off, sz in zip(k_refs, offs, sizes):
                s = jax.lax.dot_general(k_ref[0, hd], q, _CONTRACT_LAST, preferred_element_type=F32)
                s_scr[i % n_slots, off:off + sz, :] = s
                part = jnp.max(s, axis=0, keepdims=True)
                col_max = part if col_max is None else jnp.maximum(col_max, part)
            return col_max

        def attend(i, col_max):
            hd, c = items[i]
            denom = None
            out = None
            for v_ref, off, sz in zip(v_refs, offs, sizes):
                p = jnp.exp2(s_scr[i % n_slots, off:off + sz, :] - col_max)
                part = jnp.sum(p, axis=0, keepdims=True)
                denom = part if denom is None else denom + part
                acc = _dot_t(v_ref[0, hd], p.astype(BF16))
                out = acc if out is None else out + acc
            store(hd, c, out, denom)

        col_max = scores(0)
        for i in range(len(items)):
            next_max = scores(i + 1) if i + 1 < len(items) else None
            attend(i, col_max)
            col_max = next_max


def _score_bound(g_q, g_k):
    q_gain = jnp.maximum(1.0, jnp.max(jnp.abs(g_q[QK_NOPE:])))
    k_gain = jnp.maximum(jnp.max(jnp.abs(g_q[:QK_NOPE] * g_k[:QK_NOPE])), jnp.max(jnp.abs(g_k[QK_NOPE:])))
    return (QK_HEAD * Q_FOLD * BOUND_MARGIN * q_gain * k_gain).reshape(1)


def _attention(bound, q, ks, vs):
    bsz, nh, n, _ = q.shape
    n_parts = len(ks)
    chunk = min(ATTN_CHUNK, n)
    n_keys = sum(k.shape[2] for k in ks)
    heads = max(1, min(nh, ATTN_STEP_QUERIES // n))
    n_items = heads * (n // chunk)
    q_spec = pl.BlockSpec((1, heads, n, QK_PAD), lambda b, h, *_: (b, h, 0, 0))
    k_specs = [pl.BlockSpec((1, heads, k.shape[2], QK_PAD), lambda b, h, *_: (b, h, 0, 0)) for k in ks]
    v_specs = [pl.BlockSpec((1, heads, v.shape[2], V_HEAD), lambda b, h, *_: (b, h, 0, 0)) for v in vs]
    return pl.pallas_call(
        functools.partial(_attn_kernel, n_parts=n_parts),
        grid_spec=pltpu.PrefetchScalarGridSpec(
            num_scalar_prefetch=1,
            grid=(bsz, nh // heads),
            in_specs=[q_spec] + k_specs + v_specs,
            out_specs=pl.BlockSpec((1, heads * V_HEAD, n), lambda b, h, *_: (b, h, 0)),
            scratch_shapes=[pltpu.VMEM((min(2, n_items), n_keys, chunk), F32)],
        ),
        out_shape=jax.ShapeDtypeStruct((bsz, nh * V_HEAD, n), BF16),
        compiler_params=_params(2),
        name="attention",
    )(bound, q, *ks, *vs)


def _outproj_kernel(h_ref, o_ref, mod_ref, wo_ref, out_ref):
    mod = mod_ref[0, 0, 0]
    out_ref[0] = h_ref[0] + mod[2:3] * _dot_t(o_ref[0], wo_ref[0])


def _outproj(h, o, mod, layer, per_sample, w_o_all, j, tm):
    bsz, n, d = h.shape
    tile = pl.BlockSpec((1, tm, d), lambda b, t: (b, t, 0))
    o_tile = pl.BlockSpec((1, o.shape[1], tm), lambda b, t: (b, 0, t))
    return pl.pallas_call(
        _outproj_kernel,
        grid=(bsz, n // tm),
        in_specs=[tile, o_tile, _mod_spec(layer, 1, per_sample, d), _stacked(w_o_all.shape, (j,))],
        out_specs=tile,
        out_shape=jax.ShapeDtypeStruct(h.shape, F32),
        compiler_params=_params(2),
        name="attn_outproj",
    )(h, o, mod, w_o_all)


_SWAP = np.concatenate([np.arange(16, 32), np.arange(0, 16), np.arange(48, 64), np.arange(32, 48)])


def _pad_lanes(w):
    return jnp.concatenate([w, jnp.zeros_like(w)], axis=-1)


def _mla_weights(w_a, g_qa, w_uq, g_kva, w_ukv, g_q, g_k):
    kr = w_a[:, Q_LORA + KV_LORA:]
    w_a2 = jnp.concatenate(
        [w_a[:, :Q_LORA + KV_LORA], _pad_lanes(kr), _pad_lanes(kr[:, _SWAP])], axis=1)
    uq = w_uq.reshape(Q_LORA, MLA_HEADS, QK_HEAD)
    uq_rope = uq[:, :, QK_NOPE:]
    w_uq2 = jnp.concatenate(
        [uq[:, :, :QK_NOPE].reshape(Q_LORA, -1),
         _pad_lanes(uq_rope).reshape(Q_LORA, -1),
         _pad_lanes(uq_rope[:, :, _SWAP]).reshape(Q_LORA, -1)], axis=1)
    ukv = w_ukv.reshape(KV_LORA, MLA_HEADS, QK_NOPE + V_HEAD)
    w_ukv2 = jnp.concatenate(
        [ukv[:, :, :QK_NOPE].reshape(KV_LORA, -1), ukv[:, :, QK_NOPE:].reshape(KV_LORA, -1)], axis=1)

    def gains(gv):
        rope = gv[QK_NOPE:]
        return jnp.stack([gv[:QK_NOPE], _pad_lanes(rope), _pad_lanes(rope[_SWAP])])

    return {
        "w_a": w_a2.astype(BF16), "g_qa": g_qa[None], "w_uq": w_uq2.astype(BF16),
        "g_kva": g_kva[None], "w_ukv": w_ukv2.astype(BF16), "g_q": gains(g_q), "g_k": gains(g_k),
    }


def _rope_tables(n):
    pos = np.arange(n)
    n_freq = QK_ROPE // 4
    inv = ROPE_BASE ** (-np.arange(n_freq, dtype=np.float64) / n_freq)
    ang_r = (pos // GRID_W).astype(np.float64)[:, None] * inv
    ang_c = (pos % GRID_W).astype(np.float64)[:, None] * inv
    zeros = np.zeros((n, LANES - QK_ROPE), np.float64)
    cos = np.concatenate([np.cos(ang_r), np.cos(ang_r), np.cos(ang_c), np.cos(ang_c), zeros], axis=1)
    sin = np.concatenate([-np.sin(ang_r), np.sin(ang_r), -np.sin(ang_c), np.sin(ang_c), zeros], axis=1)
    return jnp.asarray(cos, F32), jnp.asarray(sin, F32)


def _identity_tables(n):
    ones = np.concatenate([np.ones((n, QK_ROPE), np.float32), np.zeros((n, LANES - QK_ROPE), np.float32)], axis=1)
    return jnp.asarray(ones, F32), jnp.zeros((n, LANES), F32)


def kernel(x, c, ctx, c_ctx, w_mod, b_mod, g_norm, ffn_w1, ffn_w3, ffn_w2, sc_w_in, sc_conv, sc_w_out,
           mla_w_a, mla_g_qa, mla_w_uq, mla_g_kva, mla_w_ukv, mla_g_q, mla_g_k, mla_w_o):
    bsz, n, d = x.shape
    n_ctx = ctx.shape[1]
    depth = w_mod.shape[0]

    cond = jnp.concatenate([c, c_ctx[None], jnp.zeros((MOD_ROWS - bsz - 1, d), F32)], axis=0)
    mod = _modulation(cond, w_mod, b_mod).reshape(depth, MOD_ROWS, 3, 3, d)

    g_all = g_norm.reshape(depth * 3, 1, d)
    w1_all, w3_all, w2_all = ffn_w1.astype(BF16), ffn_w3.astype(BF16), ffn_w2.astype(BF16)
    sc_in_all, sc_out_all = sc_w_in.astype(BF16), sc_w_out.astype(BF16)
    w_o_all = mla_w_o.astype(BF16)
    cos_x, sin_x = _rope_tables(n)
    cos_c, sin_c = _identity_tables(n_ctx)

    def ffn(h_lat, h_ctx, layer, sub, outproj=None):
        return _ffn(h_lat, h_ctx, mod, layer, sub, g_all, w1_all, w3_all, w2_all, outproj)

    h_x, h_c = x, ctx
    for i in range(depth):
        kind, j = i % 2, i // 2
        last = i == depth - 1
        run_ctx_in = (not last) or kind == 1
        run_ctx_out = not last

        if run_ctx_in:
            h_x, h_c = ffn(h_x, h_c, i, 0)
        else:
            h_x, _ = ffn(h_x, None, i, 0)

        pending_outproj = None
        if kind == 0:
            h_x = _shortconv(h_x, mod, i, True, g_all, sc_in_all, sc_conv, sc_out_all, j)
            if run_ctx_out:
                h_c = _shortconv(h_c, mod, i, False, g_all, sc_in_all, sc_conv, sc_out_all, j)
        else:
            wts = _mla_weights(mla_w_a[j], mla_g_qa[j], mla_w_uq[j], mla_g_kva[j], mla_w_ukv[j],
                               mla_g_q[j], mla_g_k[j])
            q_c, k_c, v_c = _mla_proj(h_c, mod, i, False, g_all, wts, cos_c, sin_c, n_ctx)
            q_x, k_x, v_x = _mla_proj(h_x, mod, i, True, g_all, wts, cos_x, sin_x, PROJ_TILE)
            bound = _score_bound(mla_g_q[j], mla_g_k[j])
            pending_outproj = (_attention(bound, q_x, [k_c, k_x], [v_c, v_x]), w_o_all, j)
            if run_ctx_out:
                o_c = _attention(bound, q_c, [k_c], [v_c])
                h_c = _outproj(h_c, o_c, mod, i, False, w_o_all, j, n_ctx)

        if run_ctx_out:
            h_x, h_c = ffn(h_x, h_c, i, 2, pending_outproj)
        else:
            h_x, _ = ffn(h_x, None, i, 2, pending_outproj)
    return h_x
```

```python
import functools
import math

import jax
import jax.numpy as jnp
import numpy as np
from jax.experimental import pallas as pl
from jax.experimental.pallas import tpu as pltpu

GRID_W = 64
N_MOD = 9
MLA_HEADS = 8
QK_NOPE = 128
QK_ROPE = 64
QK_HEAD = QK_NOPE + QK_ROPE
V_HEAD = 128
Q_LORA = 256
KV_LORA = 128
ROPE_BASE = 10000.0
QK_SCALE = QK_HEAD ** -0.5
Q_FOLD = QK_SCALE * math.log2(math.e)
EPS = 1e-6

LANES = 128
SUBLANES = 8
BF16_SUBLANES = 16
QK_PAD = 2 * LANES
VMEM_LIMIT = 56 * 1024 * 1024
MOD_ROWS = 24

FFN_TILE = 1024
SUB = 512
PROJ_TILE = 512
ATTN_CHUNK = 512
ATTN_STEP_QUERIES = 2048

MAX_BOUND_SHIFT = 48.0
BOUND_MARGIN = 1.02

BF16 = jnp.bfloat16
F32 = jnp.float32

_CONTRACT_LAST = (((1,), (1,)), ((), ()))
_CONTRACT_FIRST = (((0,), (0,)), ((), ()))


def _dot(a, b):
    return jnp.dot(a, b, preferred_element_type=F32)


def _dot_t(a_t, b):
    return jax.lax.dot_general(a_t, b, _CONTRACT_FIRST, preferred_element_type=F32)


def _resident(shape):
    nd = len(shape)
    return pl.BlockSpec(shape, lambda *_: (0,) * nd, pipeline_mode=pl.Buffered(1))


def _stacked(shape, lead):
    nl = len(lead)
    tail = len(shape) - nl
    return pl.BlockSpec((1,) * nl + tuple(shape[nl:]), lambda *_: tuple(lead) + (0,) * tail,
                        pipeline_mode=pl.Buffered(1))


def _params(n_grid):
    return pltpu.CompilerParams(
        dimension_semantics=("arbitrary",) * n_grid, vmem_limit_bytes=VMEM_LIMIT)


def _unit_rows(x):
    n = x.shape[-1]
    return x * jax.lax.rsqrt(jnp.sum(x * x, axis=-1, keepdims=True) + n * EPS)


def _norm_gain(gain):
    return math.sqrt(gain.shape[-1]) * gain


def _pre(h, gain_scale, shift):
    return _unit_rows(h) * gain_scale + shift


def _silu(a):
    return a * jax.nn.sigmoid(a)


def _mod_kernel(cond_ref, w_ref, b_ref, o_ref):
    cond = _silu(cond_ref[...]).astype(BF16)
    o_ref[0] = _dot(cond, w_ref[0].astype(BF16)) + b_ref[0]


def _modulation(cond, w_mod, b_mod):
    depth, d, nm = w_mod.shape
    tn = nm // 4
    return pl.pallas_call(
        _mod_kernel,
        grid=(depth, nm // tn),
        in_specs=[
            pl.BlockSpec((MOD_ROWS, d), lambda i, j: (0, 0)),
            pl.BlockSpec((1, d, tn), lambda i, j: (i, 0, j)),
            pl.BlockSpec((1, 1, tn), lambda i, j: (i, 0, j)),
        ],
        out_specs=pl.BlockSpec((1, MOD_ROWS, tn), lambda i, j: (i, 0, j)),
        out_shape=jax.ShapeDtypeStruct((depth, MOD_ROWS, nm), F32),
        compiler_params=_params(2),
        name="adaln_mod",
    )(cond, w_mod, b_mod.reshape(depth, 1, nm))


def _mod_spec(layer, sub, per_sample, d):
    if per_sample:
        return pl.BlockSpec((1, 1, 1, 3, d), lambda b, t: (layer, b, sub, 0, 0))
    ctx_row = MOD_ROWS - SUBLANES
    return pl.BlockSpec((1, 1, 1, 3, d), lambda b, t: (layer, ctx_row, sub, 0, 0))


def _gain_spec(layer, sub, d):
    return pl.BlockSpec((1, 1, d), lambda b, t: (3 * layer + sub, 0, 0))


def _ffn_kernel(*refs, fused_outproj, cast_next):
    refs = list(refs)
    h_ref = refs.pop(0)
    if fused_outproj:
        ot_ref, mix_ref, wo_ref = refs.pop(0), refs.pop(0), refs.pop(0)
    mod_ref, g_ref, w1_ref, w3_ref, w2_ref = (refs.pop(0) for _ in range(5))
    next_f32 = [refs.pop(0) for _ in range(3)] if cast_next else []
    o_ref = refs.pop(0)
    for src, dst in zip(next_f32, refs):
        dst[...] = src[0, 0].astype(BF16)
    mod = mod_ref[0, 0, 0]
    gain_scale = _norm_gain(g_ref[0]) * (1.0 + mod[1:2])
    shift = mod[0:1]
    half_gate = 0.5 * mod[2:3]
    tm = h_ref.shape[1]
    sub = min(SUB, tm)
    for s in range(tm // sub):
        rows = slice(s * sub, (s + 1) * sub)
        h = h_ref[0, rows, :]
        if fused_outproj:
            h = h + mix_ref[0, 0, 0][2:3] * _dot_t(ot_ref[0, :, rows], wo_ref[0])
        xn = _pre(h, gain_scale, shift).astype(BF16)
        a = _dot(xn, w1_ref[...])
        b = _dot(xn, w3_ref[...])
        gated = (_silu(a) * b).astype(BF16)
        o_ref[0, rows, :] = h + half_gate * _dot(gated, w2_ref[...])


def _row_chunks(n_rows, n_steps):
    for steps_per_chunk in (1, 2, 4, 8):
        rows, rem = divmod(n_rows * steps_per_chunk, n_steps)
        if rem == 0 and rows % BF16_SUBLANES == 0:
            return rows, steps_per_chunk
    raise ValueError(f"cannot split {n_rows} weight rows over {n_steps} grid steps")


def _ffn(h, mod, layer, sub, per_sample, g_all, weights, tm, outproj=None, cast_next=None):
    bsz, n, d = h.shape
    nt = n // tm
    tile = pl.BlockSpec((1, tm, d), lambda b, t: (b, t, 0))
    specs = [_mod_spec(layer, sub, per_sample, d), _gain_spec(layer, sub, d)] + [_resident(w.shape) for w in weights]
    args = [mod, g_all, *weights]
    if outproj is not None:
        o_t, w_o_all, j = outproj
        specs = [pl.BlockSpec((1, o_t.shape[1], tm), lambda b, t: (b, 0, t)),
                 _mod_spec(layer, 1, per_sample, d), _stacked(w_o_all.shape, (j,))] + specs
        args = [o_t, mod, w_o_all] + args
    out_specs = [tile]
    out_shapes = [jax.ShapeDtypeStruct(h.shape, F32)]
    if cast_next is not None:
        stacked, (nl, nw) = cast_next
        for w in stacked:
            rows, spc = _row_chunks(w.shape[2], bsz * nt)
            cols = w.shape[3]
            specs.append(pl.BlockSpec((1, 1, rows, cols),
                                      lambda b, t, spc=spc: (nl, nw, (b * nt + t) // spc, 0)))
            args.append(w)
            out_specs.append(pl.BlockSpec((rows, cols), lambda b, t, spc=spc: ((b * nt + t) // spc, 0)))
            out_shapes.append(jax.ShapeDtypeStruct(w.shape[2:], BF16))
    outs = pl.pallas_call(
        functools.partial(_ffn_kernel, fused_outproj=outproj is not None, cast_next=cast_next is not None),
        grid=(bsz, nt),
        in_specs=[tile] + specs,
        out_specs=out_specs,
        out_shape=out_shapes,
        compiler_params=_params(2),
        name="ffn",
    )(h, *args)
    return outs[0], tuple(outs[1:])


def _shortconv_kernel(h_ref, mod_ref, g_ref, win_ref, cw_ref, wout_ref, o_ref, cu_scr):
    n, d = h_ref.shape[1], h_ref.shape[2]
    sub = min(SUB, n)
    n_sub = n // sub
    mod = mod_ref[0, 0, 0]
    gain_scale = _norm_gain(g_ref[0]) * (1.0 + mod[1:2])
    shift = mod[0:1]
    gate = mod[2:3]
    cw = cw_ref[0]
    zero_row = jnp.zeros((1, d), F32)
    cu_scr[SUBLANES - 1:SUBLANES, :] = zero_row
    cu_scr[n + SUBLANES:n + SUBLANES + 1, :] = zero_row

    def project(s):
        rows = slice(s * sub, (s + 1) * sub)
        xn = _pre(h_ref[0, rows, :], gain_scale, shift).astype(BF16)
        p = _dot(xn, win_ref[0])
        cu_scr[SUBLANES + s * sub:SUBLANES + (s + 1) * sub, :] = p[:, d:2 * d] * p[:, 2 * d:]
        return p[:, :d]

    def mix(s, b_gate):
        rows = slice(s * sub, (s + 1) * sub)
        lo = SUBLANES + s * sub
        conv = (cw[0:1] * cu_scr[lo - 1:lo - 1 + sub, :] + cw[1:2] * cu_scr[lo:lo + sub, :]
                + cw[2:3] * cu_scr[lo + 1:lo + 1 + sub, :])
        y = (b_gate * conv).astype(BF16)
        o_ref[0, rows, :] = h_ref[0, rows, :] + gate * _dot(y, wout_ref[0])

    b_gate = project(0)
    for s in range(n_sub):
        b_next = project(s + 1) if s + 1 < n_sub else None
        mix(s, b_gate)
        b_gate = b_next


def _shortconv(h, mod, layer, per_sample, g_all, w_in_all, conv_all, w_out_all, j):
    bsz, n, d = h.shape
    tile = pl.BlockSpec((1, n, d), lambda b, t: (b, 0, 0))
    return pl.pallas_call(
        _shortconv_kernel,
        grid=(bsz, 1),
        in_specs=[tile, _mod_spec(layer, 1, per_sample, d), _gain_spec(layer, 1, d),
                  _stacked(w_in_all.shape, (j,)), _stacked(conv_all.shape, (j,)),
                  _stacked(w_out_all.shape, (j,))],
        out_specs=tile,
        out_shape=jax.ShapeDtypeStruct(h.shape, F32),
        scratch_shapes=[pltpu.VMEM((n + 2 * SUBLANES, d), F32)],
        compiler_params=_params(2),
        name="shortconv",
    )(h, mod, g_all, w_in_all, conv_all, w_out_all)


def _mla_proj_kernel(h_ref, mod_ref, g_ref, wa_ref, gqa_ref, wuq_ref, gkva_ref, wukv_ref,
                     gq_ref, gk_ref, cos_ref, sin_ref, q_ref, k_ref, v_ref):
    h = h_ref[0]
    mod = mod_ref[0, 0, 0]
    xn = _pre(h, _norm_gain(g_ref[0]) * (1.0 + mod[1:2]), mod[0:1]).astype(BF16)
    a = _dot(xn, wa_ref[...])
    cq = a[:, :Q_LORA]
    ckv = a[:, Q_LORA:Q_LORA + KV_LORA]
    kr = a[:, Q_LORA + KV_LORA:Q_LORA + KV_LORA + LANES]
    krs = a[:, Q_LORA + KV_LORA + LANES:]

    cqn = (_unit_rows(cq) * _norm_gain(gqa_ref[...])).astype(BF16)
    ckvn = (_unit_rows(ckv) * _norm_gain(gkva_ref[...])).astype(BF16)
    qraw = _dot(cqn, wuq_ref[...])
    kvraw = _dot(ckvn, wukv_ref[...])

    cos = cos_ref[...]
    sin = sin_ref[...]
    gq = gq_ref[...]
    gk = gk_ref[...] * (QK_HEAD * Q_FOLD)
    hw = MLA_HEADS * LANES
    nope_gain = gq[0:1] * gk[0:1]
    kr_rot = kr * (cos * gk[1:2]) + krs * (sin * gk[2:3])
    kr_sq = kr * kr
    q_cos = cos * gq[1:2]
    q_sin = sin * gq[2:3]
    norm_eps = QK_HEAD * EPS
    for hd in range(MLA_HEADS):
        lo = hd * LANES
        qn = qraw[:, lo:lo + LANES]
        qr = qraw[:, hw + lo:hw + lo + LANES]
        qrs = qraw[:, 2 * hw + lo:2 * hw + lo + LANES]
        r = jax.lax.rsqrt(jnp.sum(qn * qn + qr * qr, axis=-1, keepdims=True) + norm_eps)
        q_ref[0, hd, :, :LANES] = (qn * r).astype(BF16)
        q_ref[0, hd, :, LANES:] = ((qr * q_cos + qrs * q_sin) * r).astype(BF16)

        kn = kvraw[:, lo:lo + LANES]
        rk = jax.lax.rsqrt(jnp.sum(kn * kn + kr_sq, axis=-1, keepdims=True) + norm_eps)
        k_ref[0, hd, :, :LANES] = (kn * nope_gain * rk).astype(BF16)
        k_ref[0, hd, :, LANES:] = (kr_rot * rk).astype(BF16)
        v_ref[0, hd] = kvraw[:, hw + lo:hw + lo + LANES].astype(BF16)


def _mla_proj(h, mod, layer, per_sample, g_all, wts, cos, sin, tm):
    bsz, n, d = h.shape
    tile = pl.BlockSpec((1, tm, d), lambda b, t: (b, t, 0))
    tab = pl.BlockSpec((tm, LANES), lambda b, t: (t, 0))
    qk_spec = pl.BlockSpec((1, MLA_HEADS, tm, QK_PAD), lambda b, t: (b, 0, t, 0))
    v_spec = pl.BlockSpec((1, MLA_HEADS, tm, V_HEAD), lambda b, t: (b, 0, t, 0))
    qk_shape = jax.ShapeDtypeStruct((bsz, MLA_HEADS, n, QK_PAD), BF16)
    v_shape = jax.ShapeDtypeStruct((bsz, MLA_HEADS, n, V_HEAD), BF16)
    small = [wts["w_a"], wts["g_qa"], wts["w_uq"], wts["g_kva"], wts["w_ukv"], wts["g_q"], wts["g_k"]]
    return pl.pallas_call(
        _mla_proj_kernel,
        grid=(bsz, n // tm),
        in_specs=[tile, _mod_spec(layer, 1, per_sample, d), _gain_spec(layer, 1, d)]
                 + [_resident(w.shape) for w in small] + [tab, tab],
        out_specs=[qk_spec, qk_spec, v_spec],
        out_shape=[qk_shape, qk_shape, v_shape],
        compiler_params=_params(2),
        name="mla_proj",
    )(h, mod, g_all, *small, cos, sin)


def _attn_kernel(*refs, n_parts):
    bound_ref, q_ref = refs[0], refs[1]
    k_refs = refs[2:2 + n_parts]
    v_refs = refs[2 + n_parts:2 + 2 * n_parts]
    o_ref = refs[2 + 2 * n_parts]
    s_scr = refs[3 + 2 * n_parts]
    n_slots, _, chunk = s_scr.shape
    items = [(hd, c) for hd in range(q_ref.shape[1]) for c in range(q_ref.shape[2] // chunk)]
    sizes = [k.shape[2] for k in k_refs]
    offs = [sum(sizes[:i]) for i in range(n_parts)]

    def q_chunk(hd, c):
        return q_ref[0, hd, c * chunk:(c + 1) * chunk, :]

    def store(hd, c, out, denom):
        o_ref[0, hd * V_HEAD:(hd + 1) * V_HEAD, c * chunk:(c + 1) * chunk] = (
            out * (1.0 / denom)).astype(o_ref.dtype)

    bound = bound_ref[0]
    shift_by_bound = bound <= MAX_BOUND_SHIFT

    @pl.when(shift_by_bound)
    def _():
        for hd, c in items:
            q = q_chunk(hd, c)
            denom = None
            out = None
            for k_ref, v_ref in zip(k_refs, v_refs):
                s = jax.lax.dot_general(k_ref[0, hd], q, _CONTRACT_LAST, preferred_element_type=F32)
                p = jnp.exp2(s - bound)
                part = jnp.sum(p, axis=0, keepdims=True)
                denom = part if denom is None else denom + part
                acc = _dot_t(v_ref[0, hd], p.astype(BF16))
                out = acc if out is None else out + acc
            store(hd, c, out, denom)

    @pl.when(jnp.logical_not(shift_by_bound))
    def _():
        def scores(i):
            hd, c = items[i]
            q = q_chunk(hd, c)
            col_max = None
            for k_ref, off, sz in zip(k_refs, offs, sizes):
                s = jax.lax.dot_general(k_ref[0, hd], q, _CONTRACT_LAST, preferred_element_type=F32)
                s_scr[i % n_slots, off:off + sz, :] = s
                part = jnp.max(s, axis=0, keepdims=True)
                col_max = part if col_max is None else jnp.maximum(col_max, part)
            return col_max

        def attend(i, col_max):
            hd, c = items[i]
            denom = None
            out = None
            for v_ref, off, sz in zip(v_refs, offs, sizes):
                p = jnp.exp2(s_scr[i % n_slots, off:off + sz, :] - col_max)
                part = jnp.sum(p, axis=0, keepdims=True)
                denom = part if denom is None else denom + part
                acc = _dot_t(v_ref[0, hd], p.astype(BF16))
                out = acc if out is None else out + acc
            store(hd, c, out, denom)

        col_max = scores(0)
        for i in range(len(items)):
            next_max = scores(i + 1) if i + 1 < len(items) else None
            attend(i, col_max)
            col_max = next_max


def _score_bound(g_q, g_k):
    q_gain = jnp.maximum(1.0, jnp.max(jnp.abs(g_q[QK_NOPE:])))
    k_gain = jnp.maximum(jnp.max(jnp.abs(g_q[:QK_NOPE] * g_k[:QK_NOPE])), jnp.max(jnp.abs(g_k[QK_NOPE:])))
    return (QK_HEAD * Q_FOLD * BOUND_MARGIN * q_gain * k_gain).reshape(1)


def _attention(bound, q, ks, vs):
    bsz, nh, n, _ = q.shape
    n_parts = len(ks)
    chunk = min(ATTN_CHUNK, n)
    n_keys = sum(k.shape[2] for k in ks)
    heads = max(1, min(nh, ATTN_STEP_QUERIES // n))
    n_items = heads * (n // chunk)
    q_spec = pl.BlockSpec((1, heads, n, QK_PAD), lambda b, h, *_: (b, h, 0, 0))
    k_specs = [pl.BlockSpec((1, heads, k.shape[2], QK_PAD), lambda b, h, *_: (b, h, 0, 0)) for k in ks]
    v_specs = [pl.BlockSpec((1, heads, v.shape[2], V_HEAD), lambda b, h, *_: (b, h, 0, 0)) for v in vs]
    return pl.pallas_call(
        functools.partial(_attn_kernel, n_parts=n_parts),
        grid_spec=pltpu.PrefetchScalarGridSpec(
            num_scalar_prefetch=1,
            grid=(bsz, nh // heads),
            in_specs=[q_spec] + k_specs + v_specs,
            out_specs=pl.BlockSpec((1, heads * V_HEAD, n), lambda b, h, *_: (b, h, 0)),
            scratch_shapes=[pltpu.VMEM((min(2, n_items), n_keys, chunk), F32)],
        ),
        out_shape=jax.ShapeDtypeStruct((bsz, nh * V_HEAD, n), BF16),
        compiler_params=_params(2),
        name="attention",
    )(bound, q, *ks, *vs)


def _outproj_kernel(h_ref, o_ref, mod_ref, wo_ref, out_ref):
    mod = mod_ref[0, 0, 0]
    out_ref[0] = h_ref[0] + mod[2:3] * _dot_t(o_ref[0], wo_ref[0])


def _outproj(h, o, mod, layer, per_sample, w_o_all, j, tm):
    bsz, n, d = h.shape
    tile = pl.BlockSpec((1, tm, d), lambda b, t: (b, t, 0))
    o_tile = pl.BlockSpec((1, o.shape[1], tm), lambda b, t: (b, 0, t))
    return pl.pallas_call(
        _outproj_kernel,
        grid=(bsz, n // tm),
        in_specs=[tile, o_tile, _mod_spec(layer, 1, per_sample, d), _stacked(w_o_all.shape, (j,))],
        out_specs=tile,
        out_shape=jax.ShapeDtypeStruct(h.shape, F32),
        compiler_params=_params(2),
        name="attn_outproj",
    )(h, o, mod, w_o_all)


_SWAP = np.concatenate([np.arange(16, 32), np.arange(0, 16), np.arange(48, 64), np.arange(32, 48)])


def _pad_lanes(w):
    return jnp.concatenate([w, jnp.zeros_like(w)], axis=-1)


def _mla_weights(w_a, g_qa, w_uq, g_kva, w_ukv, g_q, g_k):
    kr = w_a[:, Q_LORA + KV_LORA:]
    w_a2 = jnp.concatenate(
        [w_a[:, :Q_LORA + KV_LORA], _pad_lanes(kr), _pad_lanes(kr[:, _SWAP])], axis=1)
    uq = w_uq.reshape(Q_LORA, MLA_HEADS, QK_HEAD)
    uq_rope = uq[:, :, QK_NOPE:]
    w_uq2 = jnp.concatenate(
        [uq[:, :, :QK_NOPE].reshape(Q_LORA, -1),
         _pad_lanes(uq_rope).reshape(Q_LORA, -1),
         _pad_lanes(uq_rope[:, :, _SWAP]).reshape(Q_LORA, -1)], axis=1)
    ukv = w_ukv.reshape(KV_LORA, MLA_HEADS, QK_NOPE + V_HEAD)
    w_ukv2 = jnp.concatenate(
        [ukv[:, :, :QK_NOPE].reshape(KV_LORA, -1), ukv[:, :, QK_NOPE:].reshape(KV_LORA, -1)], axis=1)

    def gains(gv):
        rope = gv[QK_NOPE:]
        return jnp.stack([gv[:QK_NOPE], _pad_lanes(rope), _pad_lanes(rope[_SWAP])])

    return {
        "w_a": w_a2.astype(BF16), "g_qa": g_qa[None], "w_uq": w_uq2.astype(BF16),
        "g_kva": g_kva[None], "w_ukv": w_ukv2.astype(BF16), "g_q": gains(g_q), "g_k": gains(g_k),
    }


def _rope_tables(n):
    pos = np.arange(n)
    n_freq = QK_ROPE // 4
    inv = ROPE_BASE ** (-np.arange(n_freq, dtype=np.float64) / n_freq)
    ang_r = (pos // GRID_W).astype(np.float64)[:, None] * inv
    ang_c = (pos % GRID_W).astype(np.float64)[:, None] * inv
    zeros = np.zeros((n, LANES - QK_ROPE), np.float64)
    cos = np.concatenate([np.cos(ang_r), np.cos(ang_r), np.cos(ang_c), np.cos(ang_c), zeros], axis=1)
    sin = np.concatenate([-np.sin(ang_r), np.sin(ang_r), -np.sin(ang_c), np.sin(ang_c), zeros], axis=1)
    return jnp.asarray(cos, F32), jnp.asarray(sin, F32)


def _identity_tables(n):
    ones = np.concatenate([np.ones((n, QK_ROPE), np.float32), np.zeros((n, LANES - QK_ROPE), np.float32)], axis=1)
    return jnp.asarray(ones, F32), jnp.zeros((n, LANES), F32)


def kernel(x, c, ctx, c_ctx, w_mod, b_mod, g_norm, ffn_w1, ffn_w3, ffn_w2, sc_w_in, sc_conv, sc_w_out,
           mla_w_a, mla_g_qa, mla_w_uq, mla_g_kva, mla_w_ukv, mla_g_q, mla_g_k, mla_w_o):
    bsz, n, d = x.shape
    n_ctx = ctx.shape[1]
    depth = w_mod.shape[0]

    cond = jnp.concatenate([c, c_ctx[None], jnp.zeros((MOD_ROWS - bsz - 1, d), F32)], axis=0)
    mod = _modulation(cond, w_mod, b_mod).reshape(depth, MOD_ROWS, 3, 3, d)

    g_all = g_norm.reshape(depth * 3, 1, d)
    ffn_f32 = (ffn_w1, ffn_w3, ffn_w2)
    sc_in_all, sc_out_all = sc_w_in.astype(BF16), sc_w_out.astype(BF16)
    w_o_all = mla_w_o.astype(BF16)
    cos_x, sin_x = _rope_tables(n)
    cos_c, sin_c = _identity_tables(n_ctx)

    def ffn_pair(h_lat, h_ctx, layer, sub, weights, outproj=None):
        which = sub // 2
        nxt = (layer, 1) if which == 0 else (layer + 1, 0)
        cast_next = (ffn_f32, nxt) if nxt[0] < depth else None
        h_lat, next_weights = _ffn(h_lat, mod, layer, sub, True, g_all, weights, FFN_TILE, outproj, cast_next)
        if h_ctx is not None:
            flat, _ = _ffn(h_ctx.reshape(1, bsz * n_ctx, d), mod, layer, sub, False, g_all, weights, FFN_TILE)
            h_ctx = flat.reshape(bsz, n_ctx, d)
        return h_lat, h_ctx, next_weights

    weights = tuple(w[0, 0].astype(BF16) for w in ffn_f32)
    h_x, h_c = x, ctx
    for i in range(depth):
        kind, j = i % 2, i // 2
        last = i == depth - 1
        run_ctx_in = (not last) or kind == 1
        run_ctx_out = not last

        h_x, h_c_new, weights = ffn_pair(h_x, h_c if run_ctx_in else None, i, 0, weights)
        if run_ctx_in:
            h_c = h_c_new

        pending_outproj = None
        if kind == 0:
            h_x = _shortconv(h_x, mod, i, True, g_all, sc_in_all, sc_conv, sc_out_all, j)
            if run_ctx_out:
                h_c = _shortconv(h_c, mod, i, False, g_all, sc_in_all, sc_conv, sc_out_all, j)
        else:
            wts = _mla_weights(mla_w_a[j], mla_g_qa[j], mla_w_uq[j], mla_g_kva[j], mla_w_ukv[j],
                               mla_g_q[j], mla_g_k[j])
            q_c, k_c, v_c = _mla_proj(h_c, mod, i, False, g_all, wts, cos_c, sin_c, n_ctx)
            q_x, k_x, v_x = _mla_proj(h_x, mod, i, True, g_all, wts, cos_x, sin_x, PROJ_TILE)
            bound = _score_bound(mla_g_q[j], mla_g_k[j])
            pending_outproj = (_attention(bound, q_x, [k_c, k_x], [v_c, v_x]), w_o_all, j)
            if run_ctx_out:
                o_c = _attention(bound, q_c, [k_c], [v_c])
                h_c = _outproj(h_c, o_c, mod, i, False, w_o_all, j, n_ctx)

        h_x, h_c_new, weights = ffn_pair(h_x, h_c if run_ctx_out else None, i, 2, weights, pending_outproj)
        if run_ctx_out:
            h_c = h_c_new
    return h_x
```

```python
import functools
import math

import jax
import jax.numpy as jnp
import numpy as np
from jax.experimental import pallas as pl
from jax.experimental.pallas import tpu as pltpu

GRID_W = 64
N_MOD = 9
MLA_HEADS = 8
QK_NOPE = 128
QK_ROPE = 64
QK_HEAD = QK_NOPE + QK_ROPE
V_HEAD = 128
Q_LORA = 256
KV_LORA = 128
ROPE_BASE = 10000.0
QK_SCALE = QK_HEAD ** -0.5
Q_FOLD = QK_SCALE * math.log2(math.e)
EPS = 1e-6

LANES = 128
SUBLANES = 8
BF16_SUBLANES = 16
QK_PAD = 2 * LANES
VMEM_LIMIT = 56 * 1024 * 1024
MOD_ROWS = 24

FFN_TILE = 1024
SUB = 512
PROJ_TILE = 512
ATTN_CHUNK = 1024
ATTN_STEP_QUERIES = 2048

MAX_BOUND_SHIFT = 48.0
BOUND_MARGIN = 1.02

BF16 = jnp.bfloat16
F32 = jnp.float32

_CONTRACT_LAST = (((1,), (1,)), ((), ()))
_CONTRACT_FIRST = (((0,), (0,)), ((), ()))


def _dot(a, b):
    return jnp.dot(a, b, preferred_element_type=F32)


def _dot_t(a_t, b):
    return jax.lax.dot_general(a_t, b, _CONTRACT_FIRST, preferred_element_type=F32)


def _resident(shape):
    nd = len(shape)
    return pl.BlockSpec(shape, lambda *_: (0,) * nd, pipeline_mode=pl.Buffered(1))


def _stacked(shape, lead):
    nl = len(lead)
    tail = len(shape) - nl
    return pl.BlockSpec((1,) * nl + tuple(shape[nl:]), lambda *_: tuple(lead) + (0,) * tail,
                        pipeline_mode=pl.Buffered(1))


def _params(n_grid):
    return pltpu.CompilerParams(
        dimension_semantics=("arbitrary",) * n_grid, vmem_limit_bytes=VMEM_LIMIT)


def _unit_rows(x):
    n = x.shape[-1]
    return x * jax.lax.rsqrt(jnp.sum(x * x, axis=-1, keepdims=True) + n * EPS)


def _norm_gain(gain):
    return math.sqrt(gain.shape[-1]) * gain


def _pre(h, gain_scale, shift):
    return _unit_rows(h) * gain_scale + shift


def _silu(a):
    return a * jax.nn.sigmoid(a)


def _mod_kernel(cond_ref, w_ref, b_ref, o_ref):
    cond = _silu(cond_ref[...]).astype(BF16)
    o_ref[0] = _dot(cond, w_ref[0].astype(BF16)) + b_ref[0]


def _modulation(cond, w_mod, b_mod):
    depth, d, nm = w_mod.shape
    tn = nm // 4
    return pl.pallas_call(
        _mod_kernel,
        grid=(depth, nm // tn),
        in_specs=[
            pl.BlockSpec((MOD_ROWS, d), lambda i, j: (0, 0)),
            pl.BlockSpec((1, d, tn), lambda i, j: (i, 0, j)),
            pl.BlockSpec((1, 1, tn), lambda i, j: (i, 0, j)),
        ],
        out_specs=pl.BlockSpec((1, MOD_ROWS, tn), lambda i, j: (i, 0, j)),
        out_shape=jax.ShapeDtypeStruct((depth, MOD_ROWS, nm), F32),
        compiler_params=_params(2),
        name="adaln_mod",
    )(cond, w_mod, b_mod.reshape(depth, 1, nm))


def _mod_spec(layer, sub, per_sample, d):
    if per_sample:
        return pl.BlockSpec((1, 1, 1, 3, d), lambda b, t: (layer, b, sub, 0, 0))
    ctx_row = MOD_ROWS - SUBLANES
    return pl.BlockSpec((1, 1, 1, 3, d), lambda b, t: (layer, ctx_row, sub, 0, 0))


def _gain_spec(layer, sub, d):
    return pl.BlockSpec((1, 1, d), lambda b, t: (3 * layer + sub, 0, 0))


def _ffn_kernel(*refs, fused_outproj, n_casts):
    refs = list(refs)
    h_ref = refs.pop(0)
    if fused_outproj:
        ot_ref, mix_ref, wo_ref = refs.pop(0), refs.pop(0), refs.pop(0)
    mod_ref, g_ref, w1_ref, w3_ref, w2_ref = (refs.pop(0) for _ in range(5))
    cast_srcs = [refs.pop(0) for _ in range(n_casts)]
    o_ref = refs.pop(0)
    for src, dst in zip(cast_srcs, refs):
        dst[...] = src[(0,) * (len(src.shape) - 2)].astype(BF16)
    mod = mod_ref[0, 0, 0]
    gain_scale = _norm_gain(g_ref[0]) * (1.0 + mod[1:2])
    shift = mod[0:1]
    half_gate = 0.5 * mod[2:3]
    tm = h_ref.shape[1]
    sub = min(SUB, tm)
    for s in range(tm // sub):
        rows = slice(s * sub, (s + 1) * sub)
        h = h_ref[0, rows, :]
        if fused_outproj:
            h = h + mix_ref[0, 0, 0][2:3] * _dot_t(ot_ref[0, :, rows], wo_ref[...])
        xn = _pre(h, gain_scale, shift).astype(BF16)
        a = _dot(xn, w1_ref[...])
        b = _dot(xn, w3_ref[...])
        gated = (_silu(a) * b).astype(BF16)
        o_ref[0, rows, :] = h + half_gate * _dot(gated, w2_ref[...])


def _row_chunks(n_rows, n_steps):
    for steps_per_chunk in (1, 2, 4, 8):
        rows, rem = divmod(n_rows * steps_per_chunk, n_steps)
        if rem == 0 and rows % BF16_SUBLANES == 0:
            return rows, steps_per_chunk
    raise ValueError(f"cannot split {n_rows} weight rows over {n_steps} grid steps")


def _ffn(h, mod, layer, sub, per_sample, g_all, weights, tm, outproj=None, casts=()):
    bsz, n, d = h.shape
    nt = n // tm
    tile = pl.BlockSpec((1, tm, d), lambda b, t: (b, t, 0))
    specs = [_mod_spec(layer, sub, per_sample, d), _gain_spec(layer, sub, d)] + [_resident(w.shape) for w in weights]
    args = [mod, g_all, *weights]
    if outproj is not None:
        o_t, w_o = outproj
        specs = [pl.BlockSpec((1, o_t.shape[1], tm), lambda b, t: (b, 0, t)),
                 _mod_spec(layer, 1, per_sample, d), _resident(w_o.shape)] + specs
        args = [o_t, mod, w_o] + args
    out_specs = [tile]
    out_shapes = [jax.ShapeDtypeStruct(h.shape, F32)]
    for w, lead in casts:
        n_rows, cols = w.shape[-2:]
        rows, spc = _row_chunks(n_rows, bsz * nt)
        specs.append(pl.BlockSpec((1,) * len(lead) + (rows, cols),
                                  lambda b, t, spc=spc, lead=tuple(lead): lead + ((b * nt + t) // spc, 0)))
        args.append(w)
        out_specs.append(pl.BlockSpec((rows, cols), lambda b, t, spc=spc: ((b * nt + t) // spc, 0)))
        out_shapes.append(jax.ShapeDtypeStruct((n_rows, cols), BF16))
    outs = pl.pallas_call(
        functools.partial(_ffn_kernel, fused_outproj=outproj is not None, n_casts=len(casts)),
        grid=(bsz, nt),
        in_specs=[tile] + specs,
        out_specs=out_specs,
        out_shape=out_shapes,
        compiler_params=_params(2),
        name="ffn",
    )(h, *args)
    return outs[0], tuple(outs[1:])


def _shortconv_kernel(h_ref, mod_ref, g_ref, win_ref, cw_ref, wout_ref, o_ref, cu_scr):
    n, d = h_ref.shape[1], h_ref.shape[2]
    sub = min(SUB, n)
    n_sub = n // sub
    mod = mod_ref[0, 0, 0]
    gain_scale = _norm_gain(g_ref[0]) * (1.0 + mod[1:2])
    shift = mod[0:1]
    gate = mod[2:3]
    cw = cw_ref[0]
    zero_row = jnp.zeros((1, d), F32)
    cu_scr[SUBLANES - 1:SUBLANES, :] = zero_row
    cu_scr[n + SUBLANES:n + SUBLANES + 1, :] = zero_row

    def project(s):
        rows = slice(s * sub, (s + 1) * sub)
        xn = _pre(h_ref[0, rows, :], gain_scale, shift).astype(BF16)
        p = _dot(xn, win_ref[...])
        cu_scr[SUBLANES + s * sub:SUBLANES + (s + 1) * sub, :] = p[:, d:2 * d] * p[:, 2 * d:]
        return p[:, :d]

    def mix(s, b_gate):
        rows = slice(s * sub, (s + 1) * sub)
        lo = SUBLANES + s * sub
        conv = (cw[0:1] * cu_scr[lo - 1:lo - 1 + sub, :] + cw[1:2] * cu_scr[lo:lo + sub, :]
                + cw[2:3] * cu_scr[lo + 1:lo + 1 + sub, :])
        y = (b_gate * conv).astype(BF16)
        o_ref[0, rows, :] = h_ref[0, rows, :] + gate * _dot(y, wout_ref[...])

    b_gate = project(0)
    for s in range(n_sub):
        b_next = project(s + 1) if s + 1 < n_sub else None
        mix(s, b_gate)
        b_gate = b_next


def _shortconv(h, mod, layer, per_sample, g_all, w_in, conv_all, w_out, j):
    bsz, n, d = h.shape
    tile = pl.BlockSpec((1, n, d), lambda b, t: (b, 0, 0))
    return pl.pallas_call(
        _shortconv_kernel,
        grid=(bsz, 1),
        in_specs=[tile, _mod_spec(layer, 1, per_sample, d), _gain_spec(layer, 1, d),
                  _resident(w_in.shape), _stacked(conv_all.shape, (j,)), _resident(w_out.shape)],
        out_specs=tile,
        out_shape=jax.ShapeDtypeStruct(h.shape, F32),
        scratch_shapes=[pltpu.VMEM((n + 2 * SUBLANES, d), F32)],
        compiler_params=_params(2),
        name="shortconv",
    )(h, mod, g_all, w_in, conv_all, w_out)


def _mla_proj_kernel(h_ref, mod_ref, g_ref, wa_ref, gqa_ref, wuq_ref, gkva_ref, wukv_ref,
                     gq_ref, gk_ref, cos_ref, sin_ref, q_ref, k_ref, v_ref):
    h = h_ref[0]
    mod = mod_ref[0, 0, 0]
    xn = _pre(h, _norm_gain(g_ref[0]) * (1.0 + mod[1:2]), mod[0:1]).astype(BF16)
    a = _dot(xn, wa_ref[...])
    cq = a[:, :Q_LORA]
    ckv = a[:, Q_LORA:Q_LORA + KV_LORA]
    kr = a[:, Q_LORA + KV_LORA:Q_LORA + KV_LORA + LANES]
    krs = a[:, Q_LORA + KV_LORA + LANES:]

    cqn = (_unit_rows(cq) * _norm_gain(gqa_ref[...])).astype(BF16)
    ckvn = (_unit_rows(ckv) * _norm_gain(gkva_ref[...])).astype(BF16)
    qraw = _dot(cqn, wuq_ref[...])
    kvraw = _dot(ckvn, wukv_ref[...])

    cos = cos_ref[...]
    sin = sin_ref[...]
    gq = gq_ref[...]
    gk = gk_ref[...] * (QK_HEAD * Q_FOLD)
    hw = MLA_HEADS * LANES
    nope_gain = gq[0:1] * gk[0:1]
    kr_rot = kr * (cos * gk[1:2]) + krs * (sin * gk[2:3])
    kr_sq = kr * kr
    q_cos = cos * gq[1:2]
    q_sin = sin * gq[2:3]
    norm_eps = QK_HEAD * EPS
    for hd in range(MLA_HEADS):
        lo = hd * LANES
        qn = qraw[:, lo:lo + LANES]
        qr = qraw[:, hw + lo:hw + lo + LANES]
        qrs = qraw[:, 2 * hw + lo:2 * hw + lo + LANES]
        r = jax.lax.rsqrt(jnp.sum(qn * qn + qr * qr, axis=-1, keepdims=True) + norm_eps)
        q_ref[0, hd, :, :LANES] = (qn * r).astype(BF16)
        q_ref[0, hd, :, LANES:] = ((qr * q_cos + qrs * q_sin) * r).astype(BF16)

        kn = kvraw[:, lo:lo + LANES]
        rk = jax.lax.rsqrt(jnp.sum(kn * kn + kr_sq, axis=-1, keepdims=True) + norm_eps)
        k_ref[0, hd, :, :LANES] = (kn * nope_gain * rk).astype(BF16)
        k_ref[0, hd, :, LANES:] = (kr_rot * rk).astype(BF16)
        v_ref[0, hd] = kvraw[:, hw + lo:hw + lo + LANES].astype(BF16)


def _mla_proj(h, mod, layer, per_sample, g_all, wts, cos, sin, tm):
    bsz, n, d = h.shape
    tile = pl.BlockSpec((1, tm, d), lambda b, t: (b, t, 0))
    tab = pl.BlockSpec((tm, LANES), lambda b, t: (t, 0))
    qk_spec = pl.BlockSpec((1, MLA_HEADS, tm, QK_PAD), lambda b, t: (b, 0, t, 0))
    v_spec = pl.BlockSpec((1, MLA_HEADS, tm, V_HEAD), lambda b, t: (b, 0, t, 0))
    qk_shape = jax.ShapeDtypeStruct((bsz, MLA_HEADS, n, QK_PAD), BF16)
    v_shape = jax.ShapeDtypeStruct((bsz, MLA_HEADS, n, V_HEAD), BF16)
    small = [wts["w_a"], wts["g_qa"], wts["w_uq"], wts["g_kva"], wts["w_ukv"], wts["g_q"], wts["g_k"]]
    return pl.pallas_call(
        _mla_proj_kernel,
        grid=(bsz, n // tm),
        in_specs=[tile, _mod_spec(layer, 1, per_sample, d), _gain_spec(layer, 1, d)]
                 + [_resident(w.shape) for w in small] + [tab, tab],
        out_specs=[qk_spec, qk_spec, v_spec],
        out_shape=[qk_shape, qk_shape, v_shape],
        compiler_params=_params(2),
        name="mla_proj",
    )(h, mod, g_all, *small, cos, sin)


def _attn_kernel(*refs, n_parts):
    bound_ref, q_ref = refs[0], refs[1]
    k_refs = refs[2:2 + n_parts]
    v_refs = refs[2 + n_parts:2 + 2 * n_parts]
    o_ref = refs[2 + 2 * n_parts]
    s_scr = refs[3 + 2 * n_parts]
    n_slots, _, chunk = s_scr.shape
    items = [(hd, c) for hd in range(q_ref.shape[1]) for c in range(q_ref.shape[2] // chunk)]
    sizes = [k.shape[2] for k in k_refs]
    offs = [sum(sizes[:i]) for i in range(n_parts)]

    def q_chunk(hd, c):
        return q_ref[0, hd, c * chunk:(c + 1) * chunk, :]

    def store(hd, c, out, denom):
        o_ref[0, hd * V_HEAD:(hd + 1) * V_HEAD, c * chunk:(c + 1) * chunk] = (
            out * (1.0 / denom)).astype(o_ref.dtype)

    bound = bound_ref[0]
    shift_by_bound = bound <= MAX_BOUND_SHIFT

    @pl.when(shift_by_bound)
    def _():
        for hd, c in items:
            q = q_chunk(hd, c)
            denom = None
            out = None
            for k_ref, v_ref in zip(k_refs, v_refs):
                s = jax.lax.dot_general(k_ref[0, hd], q, _CONTRACT_LAST, preferred_element_type=F32)
                p = jnp.exp2(s - bound)
                part = jnp.sum(p, axis=0, keepdims=True)
                denom = part if denom is None else denom + part
                acc = _dot_t(v_ref[0, hd], p.astype(BF16))
                out = acc if out is None else out + acc
            store(hd, c, out, denom)

    @pl.when(jnp.logical_not(shift_by_bound))
    def _():
        def scores(i):
            hd, c = items[i]
            q = q_chunk(hd, c)
            col_max = None
            for k_ref, off, sz in zip(k_refs, offs, sizes):
                s = jax.lax.dot_general(k_ref[0, hd], q, _CONTRACT_LAST, preferred_element_type=F32)
                s_scr[i % n_slots, off:off + sz, :] = s
                part = jnp.max(s, axis=0, keepdims=True)
                col_max = part if col_max is None else jnp.maximum(col_max, part)
            return col_max

        def attend(i, col_max):
            hd, c = items[i]
            denom = None
            out = None
            for v_ref, off, sz in zip(v_refs, offs, sizes):
                p = jnp.exp2(s_scr[i % n_slots, off:off + sz, :] - col_max)
                part = jnp.sum(p, axis=0, keepdims=True)
                denom = part if denom is None else denom + part
                acc = _dot_t(v_ref[0, hd], p.astype(BF16))
                out = acc if out is None else out + acc
            store(hd, c, out, denom)

        col_max = scores(0)
        for i in range(len(items)):
            next_max = scores(i + 1) if i + 1 < len(items) else None
            attend(i, col_max)
            col_max = next_max


def _score_bound(g_q, g_k):
    q_gain = jnp.maximum(1.0, jnp.max(jnp.abs(g_q[QK_NOPE:])))
    k_gain = jnp.maximum(jnp.max(jnp.abs(g_q[:QK_NOPE] * g_k[:QK_NOPE])), jnp.max(jnp.abs(g_k[QK_NOPE:])))
    return (QK_HEAD * Q_FOLD * BOUND_MARGIN * q_gain * k_gain).reshape(1)


def _attention(bound, q, ks, vs):
    bsz, nh, n, _ = q.shape
    n_parts = len(ks)
    chunk = min(ATTN_CHUNK, n)
    n_keys = sum(k.shape[2] for k in ks)
    heads = max(1, min(nh, ATTN_STEP_QUERIES // n))
    n_items = heads * (n // chunk)
    q_spec = pl.BlockSpec((1, heads, n, QK_PAD), lambda b, h, *_: (b, h, 0, 0))
    k_specs = [pl.BlockSpec((1, heads, k.shape[2], QK_PAD), lambda b, h, *_: (b, h, 0, 0)) for k in ks]
    v_specs = [pl.BlockSpec((1, heads, v.shape[2], V_HEAD), lambda b, h, *_: (b, h, 0, 0)) for v in vs]
    return pl.pallas_call(
        functools.partial(_attn_kernel, n_parts=n_parts),
        grid_spec=pltpu.PrefetchScalarGridSpec(
            num_scalar_prefetch=1,
            grid=(bsz, nh // heads),
            in_specs=[q_spec] + k_specs + v_specs,
            out_specs=pl.BlockSpec((1, heads * V_HEAD, n), lambda b, h, *_: (b, h, 0)),
            scratch_shapes=[pltpu.VMEM((min(2, n_items), n_keys, chunk), F32)],
        ),
        out_shape=jax.ShapeDtypeStruct((bsz, nh * V_HEAD, n), BF16),
        compiler_params=_params(2),
        name="attention",
    )(bound, q, *ks, *vs)


def _outproj_kernel(h_ref, o_ref, mod_ref, wo_ref, out_ref):
    mod = mod_ref[0, 0, 0]
    out_ref[0] = h_ref[0] + mod[2:3] * _dot_t(o_ref[0], wo_ref[...])


def _outproj(h, o, mod, layer, per_sample, w_o, tm):
    bsz, n, d = h.shape
    tile = pl.BlockSpec((1, tm, d), lambda b, t: (b, t, 0))
    o_tile = pl.BlockSpec((1, o.shape[1], tm), lambda b, t: (b, 0, t))
    return pl.pallas_call(
        _outproj_kernel,
        grid=(bsz, n // tm),
        in_specs=[tile, o_tile, _mod_spec(layer, 1, per_sample, d), _resident(w_o.shape)],
        out_specs=tile,
        out_shape=jax.ShapeDtypeStruct(h.shape, F32),
        compiler_params=_params(2),
        name="attn_outproj",
    )(h, o, mod, w_o)


_SWAP = np.concatenate([np.arange(16, 32), np.arange(0, 16), np.arange(48, 64), np.arange(32, 48)])


def _pad_lanes(w):
    return jnp.concatenate([w, jnp.zeros_like(w)], axis=-1)


def _mla_weights(w_a, g_qa, w_uq, g_kva, w_ukv, g_q, g_k):
    kr = w_a[:, Q_LORA + KV_LORA:]
    w_a2 = jnp.concatenate(
        [w_a[:, :Q_LORA + KV_LORA], _pad_lanes(kr), _pad_lanes(kr[:, _SWAP])], axis=1)
    uq = w_uq.reshape(Q_LORA, MLA_HEADS, QK_HEAD)
    uq_rope = uq[:, :, QK_NOPE:]
    w_uq2 = jnp.concatenate(
        [uq[:, :, :QK_NOPE].reshape(Q_LORA, -1),
         _pad_lanes(uq_rope).reshape(Q_LORA, -1),
         _pad_lanes(uq_rope[:, :, _SWAP]).reshape(Q_LORA, -1)], axis=1)
    ukv = w_ukv.reshape(KV_LORA, MLA_HEADS, QK_NOPE + V_HEAD)
    w_ukv2 = jnp.concatenate(
        [ukv[:, :, :QK_NOPE].reshape(KV_LORA, -1), ukv[:, :, QK_NOPE:].reshape(KV_LORA, -1)], axis=1)

    def gains(gv):
        rope = gv[QK_NOPE:]
        return jnp.stack([gv[:QK_NOPE], _pad_lanes(rope), _pad_lanes(rope[_SWAP])])

    return {
        "w_a": w_a2.astype(BF16), "g_qa": g_qa[None], "w_uq": w_uq2.astype(BF16),
        "g_kva": g_kva[None], "w_ukv": w_ukv2.astype(BF16), "g_q": gains(g_q), "g_k": gains(g_k),
    }


def _rope_tables(n):
    pos = np.arange(n)
    n_freq = QK_ROPE // 4
    inv = ROPE_BASE ** (-np.arange(n_freq, dtype=np.float64) / n_freq)
    ang_r = (pos // GRID_W).astype(np.float64)[:, None] * inv
    ang_c = (pos % GRID_W).astype(np.float64)[:, None] * inv
    zeros = np.zeros((n, LANES - QK_ROPE), np.float64)
    cos = np.concatenate([np.cos(ang_r), np.cos(ang_r), np.cos(ang_c), np.cos(ang_c), zeros], axis=1)
    sin = np.concatenate([-np.sin(ang_r), np.sin(ang_r), -np.sin(ang_c), np.sin(ang_c), zeros], axis=1)
    return jnp.asarray(cos, F32), jnp.asarray(sin, F32)


def _identity_tables(n):
    ones = np.concatenate([np.ones((n, QK_ROPE), np.float32), np.zeros((n, LANES - QK_ROPE), np.float32)], axis=1)
    return jnp.asarray(ones, F32), jnp.zeros((n, LANES), F32)


def kernel(x, c, ctx, c_ctx, w_mod, b_mod, g_norm, ffn_w1, ffn_w3, ffn_w2, sc_w_in, sc_conv, sc_w_out,
           mla_w_a, mla_g_qa, mla_w_uq, mla_g_kva, mla_w_ukv, mla_g_q, mla_g_k, mla_w_o):
    bsz, n, d = x.shape
    n_ctx = ctx.shape[1]
    depth = w_mod.shape[0]

    cond = jnp.concatenate([c, c_ctx[None], jnp.zeros((MOD_ROWS - bsz - 1, d), F32)], axis=0)
    mod = _modulation(cond, w_mod, b_mod).reshape(depth, MOD_ROWS, 3, 3, d)

    g_all = g_norm.reshape(depth * 3, 1, d)
    ffn_f32 = (ffn_w1, ffn_w3, ffn_w2)
    cos_x, sin_x = _rope_tables(n)
    cos_c, sin_c = _identity_tables(n_ctx)

    def ffn_pair(h_lat, h_ctx, layer, sub, weights, outproj=None, mixer_casts=()):
        which = sub // 2
        nxt = (layer, 1) if which == 0 else (layer + 1, 0)
        casts = [(w, nxt) for w in ffn_f32] if nxt[0] < depth else []
        casts += list(mixer_casts)
        h_lat, cast = _ffn(h_lat, mod, layer, sub, True, g_all, weights, FFN_TILE, outproj, casts)
        if h_ctx is not None:
            flat, _ = _ffn(h_ctx.reshape(1, bsz * n_ctx, d), mod, layer, sub, False, g_all, weights, FFN_TILE)
            h_ctx = flat.reshape(bsz, n_ctx, d)
        n_next = len(casts) - len(mixer_casts)
        return h_lat, h_ctx, cast[:n_next], cast[n_next:]

    weights = tuple(w[0, 0].astype(BF16) for w in ffn_f32)
    h_x, h_c = x, ctx
    for i in range(depth):
        kind, j = i % 2, i // 2
        last = i == depth - 1
        run_ctx_in = (not last) or kind == 1
        run_ctx_out = not last

        mixer_f32 = [(sc_w_in, (j,)), (sc_w_out, (j,))] if kind == 0 else [(mla_w_o, (j,))]
        h_x, h_c_new, weights, mixer_w = ffn_pair(h_x, h_c if run_ctx_in else None, i, 0, weights,
                                                  mixer_casts=mixer_f32)
        if run_ctx_in:
            h_c = h_c_new

        pending_outproj = None
        if kind == 0:
            w_in, w_out = mixer_w
            h_x = _shortconv(h_x, mod, i, True, g_all, w_in, sc_conv, w_out, j)
            if run_ctx_out:
                h_c = _shortconv(h_c, mod, i, False, g_all, w_in, sc_conv, w_out, j)
        else:
            (w_o,) = mixer_w
            wts = _mla_weights(mla_w_a[j], mla_g_qa[j], mla_w_uq[j], mla_g_kva[j], mla_w_ukv[j],
                               mla_g_q[j], mla_g_k[j])
            q_c, k_c, v_c = _mla_proj(h_c, mod, i, False, g_all, wts, cos_c, sin_c, n_ctx)
            q_x, k_x, v_x = _mla_proj(h_x, mod, i, True, g_all, wts, cos_x, sin_x, PROJ_TILE)
            bound = _score_bound(mla_g_q[j], mla_g_k[j])
            pending_outproj = (_attention(bound, q_x, [k_c, k_x], [v_c, v_x]), w_o)
            if run_ctx_out:
                o_c = _attention(bound, q_c, [k_c], [v_c])
                h_c = _outproj(h_c, o_c, mod, i, False, w_o, n_ctx)

        h_x, h_c_new, weights, _ = ffn_pair(h_x, h_c if run_ctx_out else None, i, 2, weights, pending_outproj)
        if run_ctx_out:
            h_c = h_c_new
    return h_x
```

```python
import functools
import math

import jax
import jax.numpy as jnp
import numpy as np
from jax.experimental import pallas as pl
from jax.experimental.pallas import tpu as pltpu

GRID_W = 64
N_MOD = 9
MLA_HEADS = 8
QK_NOPE = 128
QK_ROPE = 64
QK_HEAD = QK_NOPE + QK_ROPE
V_HEAD = 128
Q_LORA = 256
KV_LORA = 128
ROPE_BASE = 10000.0
QK_SCALE = QK_HEAD ** -0.5
Q_FOLD = QK_SCALE * math.log2(math.e)
EPS = 1e-6

LANES = 128
SUBLANES = 8
BF16_SUBLANES = 16
QK_PAD = 2 * LANES
VMEM_LIMIT = 56 * 1024 * 1024
MOD_ROWS = 24

FFN_TILE = 1024
SUB = 512
PROJ_TILE = 1024
ATTN_CHUNK = 1024
ATTN_STEP_QUERIES = 8192

MAX_BOUND_SHIFT = 48.0
BOUND_MARGIN = 1.02

BF16 = jnp.bfloat16
F32 = jnp.float32

_CONTRACT_LAST = (((1,), (1,)), ((), ()))
_CONTRACT_FIRST = (((0,), (0,)), ((), ()))


def _dot(a, b):
    return jnp.dot(a, b, preferred_element_type=F32)


def _dot_t(a_t, b):
    return jax.lax.dot_general(a_t, b, _CONTRACT_FIRST, preferred_element_type=F32)


def _resident(shape):
    nd = len(shape)
    return pl.BlockSpec(shape, lambda *_: (0,) * nd, pipeline_mode=pl.Buffered(1))


def _stacked(shape, lead):
    nl = len(lead)
    tail = len(shape) - nl
    return pl.BlockSpec((1,) * nl + tuple(shape[nl:]), lambda *_: tuple(lead) + (0,) * tail,
                        pipeline_mode=pl.Buffered(1))


def _params(n_grid):
    return pltpu.CompilerParams(
        dimension_semantics=("arbitrary",) * n_grid, vmem_limit_bytes=VMEM_LIMIT)


def _unit_rows(x):
    n = x.shape[-1]
    return x * jax.lax.rsqrt(jnp.sum(x * x, axis=-1, keepdims=True) + n * EPS)


def _norm_gain(gain):
    return math.sqrt(gain.shape[-1]) * gain


def _pre(h, gain_scale, shift):
    return _unit_rows(h) * gain_scale + shift


def _silu(a):
    return a * jax.nn.sigmoid(a)


def _mod_kernel(cond_ref, w_ref, b_ref, o_ref):
    cond = _silu(cond_ref[...]).astype(BF16)
    o_ref[0] = _dot(cond, w_ref[0].astype(BF16)) + b_ref[0]


def _modulation(cond, w_mod, b_mod):
    depth, d, nm = w_mod.shape
    tn = nm // 4
    return pl.pallas_call(
        _mod_kernel,
        grid=(depth, nm // tn),
        in_specs=[
            pl.BlockSpec((MOD_ROWS, d), lambda i, j: (0, 0)),
            pl.BlockSpec((1, d, tn), lambda i, j: (i, 0, j)),
            pl.BlockSpec((1, 1, tn), lambda i, j: (i, 0, j)),
        ],
        out_specs=pl.BlockSpec((1, MOD_ROWS, tn), lambda i, j: (i, 0, j)),
        out_shape=jax.ShapeDtypeStruct((depth, MOD_ROWS, nm), F32),
        compiler_params=_params(2),
        name="adaln_mod",
    )(cond, w_mod, b_mod.reshape(depth, 1, nm))


def _mod_spec(layer, sub, per_sample, d):
    if per_sample:
        return pl.BlockSpec((1, 1, 1, 3, d), lambda b, t: (layer, b, sub, 0, 0))
    ctx_row = MOD_ROWS - SUBLANES
    return pl.BlockSpec((1, 1, 1, 3, d), lambda b, t: (layer, ctx_row, sub, 0, 0))


def _gain_spec(layer, sub, d):
    return pl.BlockSpec((1, 1, d), lambda b, t: (3 * layer + sub, 0, 0))


def _ffn_kernel(*refs, fused_outproj, n_casts):
    refs = list(refs)
    h_ref = refs.pop(0)
    if fused_outproj:
        ot_ref, mix_ref, wo_ref = refs.pop(0), refs.pop(0), refs.pop(0)
    mod_ref, g_ref, w1_ref, w3_ref, w2_ref = (refs.pop(0) for _ in range(5))
    cast_srcs = [refs.pop(0) for _ in range(n_casts)]
    o_ref = refs.pop(0)
    for src, dst in zip(cast_srcs, refs):
        dst[...] = src[(0,) * (len(src.shape) - 2)].astype(BF16)
    mod = mod_ref[0, 0, 0]
    gain_scale = _norm_gain(g_ref[0]) * (1.0 + mod[1:2])
    shift = mod[0:1]
    half_gate = 0.5 * mod[2:3]
    tm = h_ref.shape[1]
    sub = min(SUB, tm)
    for s in range(tm // sub):
        rows = slice(s * sub, (s + 1) * sub)
        h = h_ref[0, rows, :]
        if fused_outproj:
            h = h + mix_ref[0, 0, 0][2:3] * _dot_t(ot_ref[0, :, rows], wo_ref[...])
        xn = _pre(h, gain_scale, shift).astype(BF16)
        a = _dot(xn, w1_ref[...])
        b = _dot(xn, w3_ref[...])
        gated = (_silu(a) * b).astype(BF16)
        o_ref[0, rows, :] = h + half_gate * _dot(gated, w2_ref[...])


def _row_chunks(n_rows, n_steps):
    for steps_per_chunk in (1, 2, 4, 8):
        rows, rem = divmod(n_rows * steps_per_chunk, n_steps)
        if rem == 0 and rows % BF16_SUBLANES == 0:
            return rows, steps_per_chunk
    raise ValueError(f"cannot split {n_rows} weight rows over {n_steps} grid steps")


def _ffn(h, mod, layer, sub, per_sample, g_all, weights, tm, outproj=None, casts=()):
    bsz, n, d = h.shape
    nt = n // tm
    tile = pl.BlockSpec((1, tm, d), lambda b, t: (b, t, 0))
    specs = [_mod_spec(layer, sub, per_sample, d), _gain_spec(layer, sub, d)] + [_resident(w.shape) for w in weights]
    args = [mod, g_all, *weights]
    if outproj is not None:
        o_t, w_o = outproj
        specs = [pl.BlockSpec((1, o_t.shape[1], tm), lambda b, t: (b, 0, t)),
                 _mod_spec(layer, 1, per_sample, d), _resident(w_o.shape)] + specs
        args = [o_t, mod, w_o] + args
    out_specs = [tile]
    out_shapes = [jax.ShapeDtypeStruct(h.shape, F32)]
    for w, lead in casts:
        n_rows, cols = w.shape[-2:]
        rows, spc = _row_chunks(n_rows, bsz * nt)
        specs.append(pl.BlockSpec((1,) * len(lead) + (rows, cols),
                                  lambda b, t, spc=spc, lead=tuple(lead): lead + ((b * nt + t) // spc, 0)))
        args.append(w)
        out_specs.append(pl.BlockSpec((rows, cols), lambda b, t, spc=spc: ((b * nt + t) // spc, 0)))
        out_shapes.append(jax.ShapeDtypeStruct((n_rows, cols), BF16))
    outs = pl.pallas_call(
        functools.partial(_ffn_kernel, fused_outproj=outproj is not None, n_casts=len(casts)),
        grid=(bsz, nt),
        in_specs=[tile] + specs,
        out_specs=out_specs,
        out_shape=out_shapes,
        compiler_params=_params(2),
        name="ffn",
    )(h, *args)
    return outs[0], tuple(outs[1:])


def _shortconv_kernel(h_ref, mod_ref, g_ref, win_ref, cw_ref, wout_ref, o_ref, cu_scr):
    n, d = h_ref.shape[1], h_ref.shape[2]
    sub = min(SUB, n)
    n_sub = n // sub
    mod = mod_ref[0, 0, 0]
    gain_scale = _norm_gain(g_ref[0]) * (1.0 + mod[1:2])
    shift = mod[0:1]
    gate = mod[2:3]
    cw = cw_ref[0]
    zero_row = jnp.zeros((1, d), F32)
    cu_scr[SUBLANES - 1:SUBLANES, :] = zero_row
    cu_scr[n + SUBLANES:n + SUBLANES + 1, :] = zero_row

    def project(s):
        rows = slice(s * sub, (s + 1) * sub)
        xn = _pre(h_ref[0, rows, :], gain_scale, shift).astype(BF16)
        p = _dot(xn, win_ref[...])
        cu_scr[SUBLANES + s * sub:SUBLANES + (s + 1) * sub, :] = p[:, d:2 * d] * p[:, 2 * d:]
        return p[:, :d]

    def mix(s, b_gate):
        rows = slice(s * sub, (s + 1) * sub)
        lo = SUBLANES + s * sub
        conv = (cw[0:1] * cu_scr[lo - 1:lo - 1 + sub, :] + cw[1:2] * cu_scr[lo:lo + sub, :]
                + cw[2:3] * cu_scr[lo + 1:lo + 1 + sub, :])
        y = (b_gate * conv).astype(BF16)
        o_ref[0, rows, :] = h_ref[0, rows, :] + gate * _dot(y, wout_ref[...])

    b_gate = project(0)
    for s in range(n_sub):
        b_next = project(s + 1) if s + 1 < n_sub else None
        mix(s, b_gate)
        b_gate = b_next


def _shortconv(h, mod, layer, per_sample, g_all, w_in, conv_all, w_out, j):
    bsz, n, d = h.shape
    tile = pl.BlockSpec((1, n, d), lambda b, t: (b, 0, 0))
    return pl.pallas_call(
        _shortconv_kernel,
        grid=(bsz, 1),
        in_specs=[tile, _mod_spec(layer, 1, per_sample, d), _gain_spec(layer, 1, d),
                  _resident(w_in.shape), _stacked(conv_all.shape, (j,)), _resident(w_out.shape)],
        out_specs=tile,
        out_shape=jax.ShapeDtypeStruct(h.shape, F32),
        scratch_shapes=[pltpu.VMEM((n + 2 * SUBLANES, d), F32)],
        compiler_params=_params(2),
        name="shortconv",
    )(h, mod, g_all, w_in, conv_all, w_out)


def _mla_proj_kernel(h_ref, mod_ref, g_ref, wa_ref, gqa_ref, wuq_ref, gkva_ref, wukv_ref,
                     gq_ref, gk_ref, cos_ref, sin_ref, q_ref, k_ref, v_ref):
    h = h_ref[0]
    mod = mod_ref[0, 0, 0]
    xn = _pre(h, _norm_gain(g_ref[0]) * (1.0 + mod[1:2]), mod[0:1]).astype(BF16)
    a = _dot(xn, wa_ref[...])
    cq = a[:, :Q_LORA]
    ckv = a[:, Q_LORA:Q_LORA + KV_LORA]
    kr = a[:, Q_LORA + KV_LORA:Q_LORA + KV_LORA + LANES]
    krs = a[:, Q_LORA + KV_LORA + LANES:]

    cqn = (_unit_rows(cq) * _norm_gain(gqa_ref[...])).astype(BF16)
    ckvn = (_unit_rows(ckv) * _norm_gain(gkva_ref[...])).astype(BF16)
    qraw = _dot(cqn, wuq_ref[...])
    kvraw = _dot(ckvn, wukv_ref[...])

    cos = cos_ref[...]
    sin = sin_ref[...]
    gq = gq_ref[...]
    gk = gk_ref[...] * (QK_HEAD * Q_FOLD)
    hw = MLA_HEADS * LANES
    nope_gain = gq[0:1] * gk[0:1]
    kr_rot = kr * (cos * gk[1:2]) + krs * (sin * gk[2:3])
    kr_sq = kr * kr
    q_cos = cos * gq[1:2]
    q_sin = sin * gq[2:3]
    norm_eps = QK_HEAD * EPS
    for hd in range(MLA_HEADS):
        lo = hd * LANES
        qn = qraw[:, lo:lo + LANES]
        qr = qraw[:, hw + lo:hw + lo + LANES]
        qrs = qraw[:, 2 * hw + lo:2 * hw + lo + LANES]
        r = jax.lax.rsqrt(jnp.sum(qn * qn + qr * qr, axis=-1, keepdims=True) + norm_eps)
        q_ref[0, hd, :, :LANES] = (qn * r).astype(BF16)
        q_ref[0, hd, :, LANES:] = ((qr * q_cos + qrs * q_sin) * r).astype(BF16)

        kn = kvraw[:, lo:lo + LANES]
        rk = jax.lax.rsqrt(jnp.sum(kn * kn + kr_sq, axis=-1, keepdims=True) + norm_eps)
        k_ref[0, hd, :, :LANES] = (kn * nope_gain * rk).astype(BF16)
        k_ref[0, hd, :, LANES:] = (kr_rot * rk).astype(BF16)
        v_ref[0, hd] = kvraw[:, hw + lo:hw + lo + LANES].astype(BF16)


def _mla_proj(h, mod, layer, per_sample, g_all, wts, cos, sin, tm):
    bsz, n, d = h.shape
    tile = pl.BlockSpec((1, tm, d), lambda b, t: (b, t, 0))
    tab = pl.BlockSpec((tm, LANES), lambda b, t: (t, 0))
    qk_spec = pl.BlockSpec((1, MLA_HEADS, tm, QK_PAD), lambda b, t: (b, 0, t, 0))
    v_spec = pl.BlockSpec((1, MLA_HEADS, tm, V_HEAD), lambda b, t: (b, 0, t, 0))
    qk_shape = jax.ShapeDtypeStruct((bsz, MLA_HEADS, n, QK_PAD), BF16)
    v_shape = jax.ShapeDtypeStruct((bsz, MLA_HEADS, n, V_HEAD), BF16)
    small = [wts["w_a"], wts["g_qa"], wts["w_uq"], wts["g_kva"], wts["w_ukv"], wts["g_q"], wts["g_k"]]
    return pl.pallas_call(
        _mla_proj_kernel,
        grid=(bsz, n // tm),
        in_specs=[tile, _mod_spec(layer, 1, per_sample, d), _gain_spec(layer, 1, d)]
                 + [_resident(w.shape) for w in small] + [tab, tab],
        out_specs=[qk_spec, qk_spec, v_spec],
        out_shape=[qk_shape, qk_shape, v_shape],
        compiler_params=_params(2),
        name="mla_proj",
    )(h, mod, g_all, *small, cos, sin)


def _attn_kernel(*refs, n_parts):
    bound_ref, q_ref = refs[0], refs[1]
    k_refs = refs[2:2 + n_parts]
    v_refs = refs[2 + n_parts:2 + 2 * n_parts]
    o_ref = refs[2 + 2 * n_parts]
    s_scr = refs[3 + 2 * n_parts]
    n_slots, _, chunk = s_scr.shape
    items = [(hd, c) for hd in range(q_ref.shape[1]) for c in range(q_ref.shape[2] // chunk)]
    sizes = [k.shape[2] for k in k_refs]
    offs = [sum(sizes[:i]) for i in range(n_parts)]

    def q_chunk(hd, c):
        return q_ref[0, hd, c * chunk:(c + 1) * chunk, :]

    def store(hd, c, out, denom):
        o_ref[0, hd * V_HEAD:(hd + 1) * V_HEAD, c * chunk:(c + 1) * chunk] = (
            out * (1.0 / denom)).astype(o_ref.dtype)

    bound = bound_ref[0]
    shift_by_bound = bound <= MAX_BOUND_SHIFT

    @pl.when(shift_by_bound)
    def _():
        for hd, c in items:
            q = q_chunk(hd, c)
            denom = None
            out = None
            for k_ref, v_ref in zip(k_refs, v_refs):
                s = jax.lax.dot_general(k_ref[0, hd], q, _CONTRACT_LAST, preferred_element_type=F32)
                p = jnp.exp2(s - bound)
                part = jnp.sum(p, axis=0, keepdims=True)
                denom = part if denom is None else denom + part
                acc = _dot_t(v_ref[0, hd], p.astype(BF16))
                out = acc if out is None else out + acc
            store(hd, c, out, denom)

    @pl.when(jnp.logical_not(shift_by_bound))
    def _():
        def scores(i):
            hd, c = items[i]
            q = q_chunk(hd, c)
            col_max = None
            for k_ref, off, sz in zip(k_refs, offs, sizes):
                s = jax.lax.dot_general(k_ref[0, hd], q, _CONTRACT_LAST, preferred_element_type=F32)
                s_scr[i % n_slots, off:off + sz, :] = s
                part = jnp.max(s, axis=0, keepdims=True)
                col_max = part if col_max is None else jnp.maximum(col_max, part)
            return col_max

        def attend(i, col_max):
            hd, c = items[i]
            denom = None
            out = None
            for v_ref, off, sz in zip(v_refs, offs, sizes):
                p = jnp.exp2(s_scr[i % n_slots, off:off + sz, :] - col_max)
                part = jnp.sum(p, axis=0, keepdims=True)
                denom = part if denom is None else denom + part
                acc = _dot_t(v_ref[0, hd], p.astype(BF16))
                out = acc if out is None else out + acc
            store(hd, c, out, denom)

        col_max = scores(0)
        for i in range(len(items)):
            next_max = scores(i + 1) if i + 1 < len(items) else None
            attend(i, col_max)
            col_max = next_max


def _score_bound(g_q, g_k):
    q_gain = jnp.maximum(1.0, jnp.max(jnp.abs(g_q[QK_NOPE:])))
    k_gain = jnp.maximum(jnp.max(jnp.abs(g_q[:QK_NOPE] * g_k[:QK_NOPE])), jnp.max(jnp.abs(g_k[QK_NOPE:])))
    return (QK_HEAD * Q_FOLD * BOUND_MARGIN * q_gain * k_gain).reshape(1)


def _attention(bound, q, ks, vs):
    bsz, nh, n, _ = q.shape
    n_parts = len(ks)
    chunk = min(ATTN_CHUNK, n)
    n_keys = sum(k.shape[2] for k in ks)
    heads = max(1, min(nh, ATTN_STEP_QUERIES // n))
    n_items = heads * (n // chunk)
    q_spec = pl.BlockSpec((1, heads, n, QK_PAD), lambda b, h, *_: (b, h, 0, 0))
    k_specs = [pl.BlockSpec((1, heads, k.shape[2], QK_PAD), lambda b, h, *_: (b, h, 0, 0)) for k in ks]
    v_specs = [pl.BlockSpec((1, heads, v.shape[2], V_HEAD), lambda b, h, *_: (b, h, 0, 0)) for v in vs]
    return pl.pallas_call(
        functools.partial(_attn_kernel, n_parts=n_parts),
        grid_spec=pltpu.PrefetchScalarGridSpec(
            num_scalar_prefetch=1,
            grid=(bsz, nh // heads),
            in_specs=[q_spec] + k_specs + v_specs,
            out_specs=pl.BlockSpec((1, heads * V_HEAD, n), lambda b, h, *_: (b, h, 0)),
            scratch_shapes=[pltpu.VMEM((min(2, n_items), n_keys, chunk), F32)],
        ),
        out_shape=jax.ShapeDtypeStruct((bsz, nh * V_HEAD, n), BF16),
        compiler_params=_params(2),
        name="attention",
    )(bound, q, *ks, *vs)


def _outproj_kernel(h_ref, o_ref, mod_ref, wo_ref, out_ref):
    mod = mod_ref[0, 0, 0]
    out_ref[0] = h_ref[0] + mod[2:3] * _dot_t(o_ref[0], wo_ref[...])


def _outproj(h, o, mod, layer, per_sample, w_o, tm):
    bsz, n, d = h.shape
    tile = pl.BlockSpec((1, tm, d), lambda b, t: (b, t, 0))
    o_tile = pl.BlockSpec((1, o.shape[1], tm), lambda b, t: (b, 0, t))
    return pl.pallas_call(
        _outproj_kernel,
        grid=(bsz, n // tm),
        in_specs=[tile, o_tile, _mod_spec(layer, 1, per_sample, d), _resident(w_o.shape)],
        out_specs=tile,
        out_shape=jax.ShapeDtypeStruct(h.shape, F32),
        compiler_params=_params(2),
        name="attn_outproj",
    )(h, o, mod, w_o)


_SWAP = np.concatenate([np.arange(16, 32), np.arange(0, 16), np.arange(48, 64), np.arange(32, 48)])


def _pad_lanes(w):
    return jnp.concatenate([w, jnp.zeros_like(w)], axis=-1)


def _mla_weights(w_a, g_qa, w_uq, g_kva, w_ukv, g_q, g_k):
    kr = w_a[:, Q_LORA + KV_LORA:]
    w_a2 = jnp.concatenate(
        [w_a[:, :Q_LORA + KV_LORA], _pad_lanes(kr), _pad_lanes(kr[:, _SWAP])], axis=1)
    uq = w_uq.reshape(Q_LORA, MLA_HEADS, QK_HEAD)
    uq_rope = uq[:, :, QK_NOPE:]
    w_uq2 = jnp.concatenate(
        [uq[:, :, :QK_NOPE].reshape(Q_LORA, -1),
         _pad_lanes(uq_rope).reshape(Q_LORA, -1),
         _pad_lanes(uq_rope[:, :, _SWAP]).reshape(Q_LORA, -1)], axis=1)
    ukv = w_ukv.reshape(KV_LORA, MLA_HEADS, QK_NOPE + V_HEAD)
    w_ukv2 = jnp.concatenate(
        [ukv[:, :, :QK_NOPE].reshape(KV_LORA, -1), ukv[:, :, QK_NOPE:].reshape(KV_LORA, -1)], axis=1)

    def gains(gv):
        rope = gv[QK_NOPE:]
        return jnp.stack([gv[:QK_NOPE], _pad_lanes(rope), _pad_lanes(rope[_SWAP])])

    return {
        "w_a": w_a2.astype(BF16), "g_qa": g_qa[None], "w_uq": w_uq2.astype(BF16),
        "g_kva": g_kva[None], "w_ukv": w_ukv2.astype(BF16), "g_q": gains(g_q), "g_k": gains(g_k),
    }


def _rope_tables(n):
    pos = np.arange(n)
    n_freq = QK_ROPE // 4
    inv = ROPE_BASE ** (-np.arange(n_freq, dtype=np.float64) / n_freq)
    ang_r = (pos // GRID_W).astype(np.float64)[:, None] * inv
    ang_c = (pos % GRID_W).astype(np.float64)[:, None] * inv
    zeros = np.zeros((n, LANES - QK_ROPE), np.float64)
    cos = np.concatenate([np.cos(ang_r), np.cos(ang_r), np.cos(ang_c), np.cos(ang_c), zeros], axis=1)
    sin = np.concatenate([-np.sin(ang_r), np.sin(ang_r), -np.sin(ang_c), np.sin(ang_c), zeros], axis=1)
    return jnp.asarray(cos, F32), jnp.asarray(sin, F32)


def _identity_tables(n):
    ones = np.concatenate([np.ones((n, QK_ROPE), np.float32), np.zeros((n, LANES - QK_ROPE), np.float32)], axis=1)
    return jnp.asarray(ones, F32), jnp.zeros((n, LANES), F32)


def kernel(x, c, ctx, c_ctx, w_mod, b_mod, g_norm, ffn_w1, ffn_w3, ffn_w2, sc_w_in, sc_conv, sc_w_out,
           mla_w_a, mla_g_qa, mla_w_uq, mla_g_kva, mla_w_ukv, mla_g_q, mla_g_k, mla_w_o):
    bsz, n, d = x.shape
    n_ctx = ctx.shape[1]
    depth = w_mod.shape[0]

    cond = jnp.concatenate([c, c_ctx[None], jnp.zeros((MOD_ROWS - bsz - 1, d), F32)], axis=0)
    mod = _modulation(cond, w_mod, b_mod).reshape(depth, MOD_ROWS, 3, 3, d)

    g_all = g_norm.reshape(depth * 3, 1, d)
    ffn_f32 = (ffn_w1, ffn_w3, ffn_w2)
    cos_x, sin_x = _rope_tables(n)
    cos_c, sin_c = _identity_tables(n_ctx)

    def ffn_pair(h_lat, h_ctx, layer, sub, weights, outproj=None, mixer_casts=()):
        which = sub // 2
        nxt = (layer, 1) if which == 0 else (layer + 1, 0)
        casts = [(w, nxt) for w in ffn_f32] if nxt[0] < depth else []
        casts += list(mixer_casts)
        h_lat, cast = _ffn(h_lat, mod, layer, sub, True, g_all, weights, FFN_TILE, outproj, casts)
        if h_ctx is not None:
            flat, _ = _ffn(h_ctx.reshape(1, bsz * n_ctx, d), mod, layer, sub, False, g_all, weights, FFN_TILE)
            h_ctx = flat.reshape(bsz, n_ctx, d)
        n_next = len(casts) - len(mixer_casts)
        return h_lat, h_ctx, cast[:n_next], cast[n_next:]

    weights = tuple(w[0, 0].astype(BF16) for w in ffn_f32)
    h_x, h_c = x, ctx
    for i in range(depth):
        kind, j = i % 2, i // 2
        last = i == depth - 1
        run_ctx_in = (not last) or kind == 1
        run_ctx_out = not last

        mixer_f32 = [(sc_w_in, (j,)), (sc_w_out, (j,))] if kind == 0 else [(mla_w_o, (j,))]
        h_x, h_c_new, weights, mixer_w = ffn_pair(h_x, h_c if run_ctx_in else None, i, 0, weights,
                                                  mixer_casts=mixer_f32)
        if run_ctx_in:
            h_c = h_c_new

        pending_outproj = None
        if kind == 0:
            w_in, w_out = mixer_w
            h_x = _shortconv(h_x, mod, i, True, g_all, w_in, sc_conv, w_out, j)
            if run_ctx_out:
                h_c = _shortconv(h_c, mod, i, False, g_all, w_in, sc_conv, w_out, j)
        else:
            (w_o,) = mixer_w
            wts = _mla_weights(mla_w_a[j], mla_g_qa[j], mla_w_uq[j], mla_g_kva[j], mla_w_ukv[j],
                               mla_g_q[j], mla_g_k[j])
            q_c, k_c, v_c = _mla_proj(h_c, mod, i, False, g_all, wts, cos_c, sin_c, n_ctx)
            q_x, k_x, v_x = _mla_proj(h_x, mod, i, True, g_all, wts, cos_x, sin_x, PROJ_TILE)
            bound = _score_bound(mla_g_q[j], mla_g_k[j])
            pending_outproj = (_attention(bound, q_x, [k_c, k_x], [v_c, v_x]), w_o)
            if run_ctx_out:
                o_c = _attention(bound, q_c, [k_c], [v_c])
                h_c = _outproj(h_c, o_c, mod, i, False, w_o, n_ctx)

        h_x, h_c_new, weights, _ = ffn_pair(h_x, h_c if run_ctx_out else None, i, 2, weights, pending_outproj)
        if run_ctx_out:
            h_c = h_c_new
    return h_x
```

```python
import functools
import math

import jax
import jax.numpy as jnp
import numpy as np
from jax.experimental import pallas as pl
from jax.experimental.pallas import tpu as pltpu

GRID_W = 64
MLA_HEADS = 8
QK_NOPE = 128
QK_ROPE = 64
QK_HEAD = QK_NOPE + QK_ROPE
V_HEAD = 128
Q_LORA = 256
KV_LORA = 128
ROPE_BASE = 10000.0
QK_SCALE = QK_HEAD ** -0.5
Q_FOLD = QK_SCALE * math.log2(math.e)
EPS = 1e-6

LANES = 128
SUBLANES = 8
BF16_SUBLANES = 16
QK_PAD = 2 * LANES
VMEM_LIMIT = 56 * 1024 * 1024
MOD_ROWS = 24
MOD_COL_BLOCKS = 4

FFN_TILE = 1024
SUB = 256
PROJ_TILE = 1024
ATTN_CHUNK = 1024
ATTN_STEP_QUERIES = 8192

MAX_BOUND_SHIFT = 48.0
BOUND_MARGIN = 1.02

BF16 = jnp.bfloat16
F32 = jnp.float32

_CONTRACT_LAST = (((1,), (1,)), ((), ()))
_CONTRACT_FIRST = (((0,), (0,)), ((), ()))


def _dot(a, b):
    return jnp.dot(a, b, preferred_element_type=F32)


def _dot_t(a_t, b):
    return jax.lax.dot_general(a_t, b, _CONTRACT_FIRST, preferred_element_type=F32)


def _resident(shape):
    nd = len(shape)
    return pl.BlockSpec(shape, lambda *_: (0,) * nd, pipeline_mode=pl.Buffered(1))


def _stacked(shape, lead):
    nl = len(lead)
    tail = len(shape) - nl
    return pl.BlockSpec((1,) * nl + tuple(shape[nl:]), lambda *_: tuple(lead) + (0,) * tail,
                        pipeline_mode=pl.Buffered(1))


def _params(n_grid):
    return pltpu.CompilerParams(
        dimension_semantics=("arbitrary",) * n_grid, vmem_limit_bytes=VMEM_LIMIT)


def _unit_rows(x):
    n = x.shape[-1]
    return x * jax.lax.rsqrt(jnp.sum(x * x, axis=-1, keepdims=True) + n * EPS)


def _norm_gain(gain):
    return math.sqrt(gain.shape[-1]) * gain


def _pre(h, gain_scale, shift):
    return _unit_rows(h) * gain_scale + shift


def _silu(a):
    return a * jax.nn.sigmoid(a)


def _mod_kernel(cond_ref, w_ref, b_ref, o_ref):
    cond = _silu(cond_ref[...]).astype(BF16)
    o_ref[0] = _dot(cond, w_ref[0].astype(BF16)) + b_ref[0]


def _modulation(cond, w_mod, b_mod):
    depth, d, nm = w_mod.shape
    tn = nm // MOD_COL_BLOCKS
    return pl.pallas_call(
        _mod_kernel,
        grid=(depth, nm // tn),
        in_specs=[
            pl.BlockSpec((MOD_ROWS, d), lambda i, j: (0, 0)),
            pl.BlockSpec((1, d, tn), lambda i, j: (i, 0, j)),
            pl.BlockSpec((1, 1, tn), lambda i, j: (i, 0, j)),
        ],
        out_specs=pl.BlockSpec((1, MOD_ROWS, tn), lambda i, j: (i, 0, j)),
        out_shape=jax.ShapeDtypeStruct((depth, MOD_ROWS, nm), F32),
        compiler_params=_params(2),
        name="adaln_mod",
    )(cond, w_mod, b_mod.reshape(depth, 1, nm))


def _mod_spec(layer, sub, per_sample, d):
    if per_sample:
        return pl.BlockSpec((1, 1, 1, 3, d), lambda b, t: (layer, b, sub, 0, 0))
    ctx_row = MOD_ROWS - SUBLANES
    return pl.BlockSpec((1, 1, 1, 3, d), lambda b, t: (layer, ctx_row, sub, 0, 0))


def _gain_spec(layer, sub, d):
    return pl.BlockSpec((1, 1, d), lambda b, t: (3 * layer + sub, 0, 0))


def _ffn_kernel(*refs, fused_outproj, n_casts):
    refs = list(refs)
    h_ref = refs.pop(0)
    if fused_outproj:
        ot_ref, mix_ref, wo_ref = refs.pop(0), refs.pop(0), refs.pop(0)
    mod_ref, g_ref, w1_ref, w3_ref, w2_ref = (refs.pop(0) for _ in range(5))
    cast_srcs = [refs.pop(0) for _ in range(n_casts)]
    o_ref = refs.pop(0)
    for src, dst in zip(cast_srcs, refs):
        dst[...] = src[(0,) * (len(src.shape) - 2)].astype(BF16)
    mod = mod_ref[0, 0, 0]
    gain_scale = _norm_gain(g_ref[0]) * (1.0 + mod[1:2])
    shift = mod[0:1]
    half_gate = 0.5 * mod[2:3]
    tm = h_ref.shape[1]
    sub = min(SUB, tm)
    for s in range(tm // sub):
        rows = slice(s * sub, (s + 1) * sub)
        h = h_ref[0, rows, :]
        if fused_outproj:
            h = h + mix_ref[0, 0, 0][2:3] * _dot_t(ot_ref[0, :, rows], wo_ref[...])
        xn = _pre(h, gain_scale, shift).astype(BF16)
        a = _dot(xn, w1_ref[...])
        b = _dot(xn, w3_ref[...])
        gated = (_silu(a) * b).astype(BF16)
        o_ref[0, rows, :] = h + half_gate * _dot(gated, w2_ref[...])


def _row_chunks(n_rows, n_steps):
    for steps_per_chunk in (1, 2, 4, 8):
        rows, rem = divmod(n_rows * steps_per_chunk, n_steps)
        if rem == 0 and rows % BF16_SUBLANES == 0:
            return rows, steps_per_chunk
    raise ValueError(f"cannot split {n_rows} weight rows over {n_steps} grid steps")


def _ffn(h, mod, layer, sub, per_sample, g_all, weights, tm, outproj=None, casts=()):
    bsz, n, d = h.shape
    nt = n // tm
    tile = pl.BlockSpec((1, tm, d), lambda b, t: (b, t, 0))
    specs = [_mod_spec(layer, sub, per_sample, d), _gain_spec(layer, sub, d)] + [_resident(w.shape) for w in weights]
    args = [mod, g_all, *weights]
    if outproj is not None:
        o_t, w_o = outproj
        specs = [pl.BlockSpec((1, o_t.shape[1], tm), lambda b, t: (b, 0, t)),
                 _mod_spec(layer, 1, per_sample, d), _resident(w_o.shape)] + specs
        args = [o_t, mod, w_o] + args
    out_specs = [tile]
    out_shapes = [jax.ShapeDtypeStruct(h.shape, F32)]
    for w, lead in casts:
        n_rows, cols = w.shape[-2:]
        rows, spc = _row_chunks(n_rows, bsz * nt)
        specs.append(pl.BlockSpec((1,) * len(lead) + (rows, cols),
                                  lambda b, t, spc=spc, lead=tuple(lead): lead + ((b * nt + t) // spc, 0)))
        args.append(w)
        out_specs.append(pl.BlockSpec((rows, cols), lambda b, t, spc=spc: ((b * nt + t) // spc, 0)))
        out_shapes.append(jax.ShapeDtypeStruct((n_rows, cols), BF16))
    outs = pl.pallas_call(
        functools.partial(_ffn_kernel, fused_outproj=outproj is not None, n_casts=len(casts)),
        grid=(bsz, nt),
        in_specs=[tile] + specs,
        out_specs=out_specs,
        out_shape=out_shapes,
        compiler_params=_params(2),
        name="ffn",
    )(h, *args)
    return outs[0], tuple(outs[1:])


def _shortconv_kernel(h_ref, mod_ref, g_ref, win_ref, cw_ref, wout_ref, o_ref, cu_scr):
    n, d = h_ref.shape[1], h_ref.shape[2]
    sub = min(SUB, n)
    n_sub = n // sub
    mod = mod_ref[0, 0, 0]
    gain_scale = _norm_gain(g_ref[0]) * (1.0 + mod[1:2])
    shift = mod[0:1]
    gate = mod[2:3]
    cw = cw_ref[0]
    zero_row = jnp.zeros((1, d), F32)
    cu_scr[SUBLANES - 1:SUBLANES, :] = zero_row
    cu_scr[n + SUBLANES:n + SUBLANES + 1, :] = zero_row

    def project(s):
        rows = slice(s * sub, (s + 1) * sub)
        xn = _pre(h_ref[0, rows, :], gain_scale, shift).astype(BF16)
        p = _dot(xn, win_ref[...])
        cu_scr[SUBLANES + s * sub:SUBLANES + (s + 1) * sub, :] = p[:, d:2 * d] * p[:, 2 * d:]
        return p[:, :d]

    def mix(s, b_gate):
        rows = slice(s * sub, (s + 1) * sub)
        lo = SUBLANES + s * sub
        conv = (cw[0:1] * cu_scr[lo - 1:lo - 1 + sub, :] + cw[1:2] * cu_scr[lo:lo + sub, :]
                + cw[2:3] * cu_scr[lo + 1:lo + 1 + sub, :])
        y = (b_gate * conv).astype(BF16)
        o_ref[0, rows, :] = h_ref[0, rows, :] + gate * _dot(y, wout_ref[...])

    b_gate = project(0)
    for s in range(n_sub):
        b_next = project(s + 1) if s + 1 < n_sub else None
        mix(s, b_gate)
        b_gate = b_next


def _shortconv(h, mod, layer, per_sample, g_all, w_in, conv_all, w_out, j):
    bsz, n, d = h.shape
    tile = pl.BlockSpec((1, n, d), lambda b, t: (b, 0, 0))
    return pl.pallas_call(
        _shortconv_kernel,
        grid=(bsz, 1),
        in_specs=[tile, _mod_spec(layer, 1, per_sample, d), _gain_spec(layer, 1, d),
                  _resident(w_in.shape), _stacked(conv_all.shape, (j,)), _resident(w_out.shape)],
        out_specs=tile,
        out_shape=jax.ShapeDtypeStruct(h.shape, F32),
        scratch_shapes=[pltpu.VMEM((n + 2 * SUBLANES, d), F32)],
        compiler_params=_params(2),
        name="shortconv",
    )(h, mod, g_all, w_in, conv_all, w_out)


def _mla_proj_kernel(h_ref, mod_ref, g_ref, wa_ref, gqa_ref, wuq_ref, gkva_ref, wukv_ref,
                     gq_ref, gk_ref, cos_ref, sin_ref, q_ref, k_ref, v_ref):
    h = h_ref[0]
    mod = mod_ref[0, 0, 0]
    xn = _pre(h, _norm_gain(g_ref[0]) * (1.0 + mod[1:2]), mod[0:1]).astype(BF16)
    a = _dot(xn, wa_ref[...])
    cq = a[:, :Q_LORA]
    ckv = a[:, Q_LORA:Q_LORA + KV_LORA]
    kr = a[:, Q_LORA + KV_LORA:Q_LORA + KV_LORA + LANES]
    krs = a[:, Q_LORA + KV_LORA + LANES:]

    cqn = (_unit_rows(cq) * _norm_gain(gqa_ref[...])).astype(BF16)
    ckvn = (_unit_rows(ckv) * _norm_gain(gkva_ref[...])).astype(BF16)
    qraw = _dot(cqn, wuq_ref[...])
    kvraw = _dot(ckvn, wukv_ref[...])

    cos = cos_ref[...]
    sin = sin_ref[...]
    gq = gq_ref[...]
    gk = gk_ref[...] * (QK_HEAD * Q_FOLD)
    hw = MLA_HEADS * LANES
    nope_gain = gq[0:1] * gk[0:1]
    kr_rot = kr * (cos * gk[1:2]) + krs * (sin * gk[2:3])
    kr_sq = kr * kr
    q_cos = cos * gq[1:2]
    q_sin = sin * gq[2:3]
    norm_eps = QK_HEAD * EPS
    for hd in range(MLA_HEADS):
        lo = hd * LANES
        qn = qraw[:, lo:lo + LANES]
        qr = qraw[:, hw + lo:hw + lo + LANES]
        qrs = qraw[:, 2 * hw + lo:2 * hw + lo + LANES]
        r = jax.lax.rsqrt(jnp.sum(qn * qn + qr * qr, axis=-1, keepdims=True) + norm_eps)
        q_ref[0, hd, :, :LANES] = (qn * r).astype(BF16)
        q_ref[0, hd, :, LANES:] = ((qr * q_cos + qrs * q_sin) * r).astype(BF16)

        kn = kvraw[:, lo:lo + LANES]
        rk = jax.lax.rsqrt(jnp.sum(kn * kn + kr_sq, axis=-1, keepdims=True) + norm_eps)
        k_ref[0, hd, :, :LANES] = (kn * nope_gain * rk).astype(BF16)
        k_ref[0, hd, :, LANES:] = (kr_rot * rk).astype(BF16)
        v_ref[0, hd] = kvraw[:, hw + lo:hw + lo + LANES].astype(BF16)


def _mla_proj(h, mod, layer, per_sample, g_all, wts, cos, sin, tm):
    bsz, n, d = h.shape
    tile = pl.BlockSpec((1, tm, d), lambda b, t: (b, t, 0))
    tab = pl.BlockSpec((tm, LANES), lambda b, t: (t, 0))
    qk_spec = pl.BlockSpec((1, MLA_HEADS, tm, QK_PAD), lambda b, t: (b, 0, t, 0))
    v_spec = pl.BlockSpec((1, MLA_HEADS, tm, V_HEAD), lambda b, t: (b, 0, t, 0))
    qk_shape = jax.ShapeDtypeStruct((bsz, MLA_HEADS, n, QK_PAD), BF16)
    v_shape = jax.ShapeDtypeStruct((bsz, MLA_HEADS, n, V_HEAD), BF16)
    small = [wts["w_a"], wts["g_qa"], wts["w_uq"], wts["g_kva"], wts["w_ukv"], wts["g_q"], wts["g_k"]]
    return pl.pallas_call(
        _mla_proj_kernel,
        grid=(bsz, n // tm),
        in_specs=[tile, _mod_spec(layer, 1, per_sample, d), _gain_spec(layer, 1, d)]
                 + [_resident(w.shape) for w in small] + [tab, tab],
        out_specs=[qk_spec, qk_spec, v_spec],
        out_shape=[qk_shape, qk_shape, v_shape],
        compiler_params=_params(2),
        name="mla_proj",
    )(h, mod, g_all, *small, cos, sin)


def _attn_kernel(*refs, n_parts):
    bound_ref, q_ref = refs[0], refs[1]
    k_refs = refs[2:2 + n_parts]
    v_refs = refs[2 + n_parts:2 + 2 * n_parts]
    o_ref = refs[2 + 2 * n_parts]
    s_scr = refs[3 + 2 * n_parts]
    n_slots, _, chunk = s_scr.shape
    items = [(hd, c) for hd in range(q_ref.shape[1]) for c in range(q_ref.shape[2] // chunk)]
    sizes = [k.shape[2] for k in k_refs]
    offs = [sum(sizes[:i]) for i in range(n_parts)]

    def q_chunk(hd, c):
        return q_ref[0, hd, c * chunk:(c + 1) * chunk, :]

    def store(hd, c, out, denom):
        o_ref[0, hd * V_HEAD:(hd + 1) * V_HEAD, c * chunk:(c + 1) * chunk] = (
            out * (1.0 / denom)).astype(o_ref.dtype)

    bound = bound_ref[0]
    shift_by_bound = bound <= MAX_BOUND_SHIFT

    @pl.when(shift_by_bound)
    def _():
        for hd, c in items:
            q = q_chunk(hd, c)
            denom = None
            out = None
            for k_ref, v_ref in zip(k_refs, v_refs):
                s = jax.lax.dot_general(k_ref[0, hd], q, _CONTRACT_LAST, preferred_element_type=F32)
                p = jnp.exp2(s - bound)
                part = jnp.sum(p, axis=0, keepdims=True)
                denom = part if denom is None else denom + part
                acc = _dot_t(v_ref[0, hd], p.astype(BF16))
                out = acc if out is None else out + acc
            store(hd, c, out, denom)

    @pl.when(jnp.logical_not(shift_by_bound))
    def _():
        def scores(i):
            hd, c = items[i]
            q = q_chunk(hd, c)
            col_max = None
            for k_ref, off, sz in zip(k_refs, offs, sizes):
                s = jax.lax.dot_general(k_ref[0, hd], q, _CONTRACT_LAST, preferred_element_type=F32)
                s_scr[i % n_slots, off:off + sz, :] = s
                part = jnp.max(s, axis=0, keepdims=True)
                col_max = part if col_max is None else jnp.maximum(col_max, part)
            return col_max

        def attend(i, col_max):
            hd, c = items[i]
            denom = None
            out = None
            for v_ref, off, sz in zip(v_refs, offs, sizes):
                p = jnp.exp2(s_scr[i % n_slots, off:off + sz, :] - col_max)
                part = jnp.sum(p, axis=0, keepdims=True)
                denom = part if denom is None else denom + part
                acc = _dot_t(v_ref[0, hd], p.astype(BF16))
                out = acc if out is None else out + acc
            store(hd, c, out, denom)

        col_max = scores(0)
        for i in range(len(items)):
            next_max = scores(i + 1) if i + 1 < len(items) else None
            attend(i, col_max)
            col_max = next_max


def _score_bound(g_q, g_k):
    q_gain = jnp.maximum(1.0, jnp.max(jnp.abs(g_q[QK_NOPE:])))
    k_gain = jnp.maximum(jnp.max(jnp.abs(g_q[:QK_NOPE] * g_k[:QK_NOPE])), jnp.max(jnp.abs(g_k[QK_NOPE:])))
    return (QK_HEAD * Q_FOLD * BOUND_MARGIN * q_gain * k_gain).reshape(1)


def _attention(bound, q, ks, vs):
    bsz, nh, n, _ = q.shape
    n_parts = len(ks)
    chunk = min(ATTN_CHUNK, n)
    n_keys = sum(k.shape[2] for k in ks)
    heads = max(1, min(nh, ATTN_STEP_QUERIES // n))
    n_items = heads * (n // chunk)
    q_spec = pl.BlockSpec((1, heads, n, QK_PAD), lambda b, h, *_: (b, h, 0, 0))
    k_specs = [pl.BlockSpec((1, heads, k.shape[2], QK_PAD), lambda b, h, *_: (b, h, 0, 0)) for k in ks]
    v_specs = [pl.BlockSpec((1, heads, v.shape[2], V_HEAD), lambda b, h, *_: (b, h, 0, 0)) for v in vs]
    return pl.pallas_call(
        functools.partial(_attn_kernel, n_parts=n_parts),
        grid_spec=pltpu.PrefetchScalarGridSpec(
            num_scalar_prefetch=1,
            grid=(bsz, nh // heads),
            in_specs=[q_spec] + k_specs + v_specs,
            out_specs=pl.BlockSpec((1, heads * V_HEAD, n), lambda b, h, *_: (b, h, 0)),
            scratch_shapes=[pltpu.VMEM((min(2, n_items), n_keys, chunk), F32)],
        ),
        out_shape=jax.ShapeDtypeStruct((bsz, nh * V_HEAD, n), BF16),
        compiler_params=_params(2),
        name="attention",
    )(bound, q, *ks, *vs)


def _outproj_kernel(h_ref, o_ref, mod_ref, wo_ref, out_ref):
    mod = mod_ref[0, 0, 0]
    out_ref[0] = h_ref[0] + mod[2:3] * _dot_t(o_ref[0], wo_ref[...])


def _outproj(h, o, mod, layer, per_sample, w_o, tm):
    bsz, n, d = h.shape
    tile = pl.BlockSpec((1, tm, d), lambda b, t: (b, t, 0))
    o_tile = pl.BlockSpec((1, o.shape[1], tm), lambda b, t: (b, 0, t))
    return pl.pallas_call(
        _outproj_kernel,
        grid=(bsz, n // tm),
        in_specs=[tile, o_tile, _mod_spec(layer, 1, per_sample, d), _resident(w_o.shape)],
        out_specs=tile,
        out_shape=jax.ShapeDtypeStruct(h.shape, F32),
        compiler_params=_params(2),
        name="attn_outproj",
    )(h, o, mod, w_o)


_SWAP = np.concatenate([np.arange(16, 32), np.arange(0, 16), np.arange(48, 64), np.arange(32, 48)])


def _pad_lanes(w):
    return jnp.concatenate([w, jnp.zeros_like(w)], axis=-1)


def _mla_weights(w_a, g_qa, w_uq, g_kva, w_ukv, g_q, g_k):
    kr = w_a[:, Q_LORA + KV_LORA:]
    w_a2 = jnp.concatenate(
        [w_a[:, :Q_LORA + KV_LORA], _pad_lanes(kr), _pad_lanes(kr[:, _SWAP])], axis=1)
    uq = w_uq.reshape(Q_LORA, MLA_HEADS, QK_HEAD)
    uq_rope = uq[:, :, QK_NOPE:]
    w_uq2 = jnp.concatenate(
        [uq[:, :, :QK_NOPE].reshape(Q_LORA, -1),
         _pad_lanes(uq_rope).reshape(Q_LORA, -1),
         _pad_lanes(uq_rope[:, :, _SWAP]).reshape(Q_LORA, -1)], axis=1)
    ukv = w_ukv.reshape(KV_LORA, MLA_HEADS, QK_NOPE + V_HEAD)
    w_ukv2 = jnp.concatenate(
        [ukv[:, :, :QK_NOPE].reshape(KV_LORA, -1), ukv[:, :, QK_NOPE:].reshape(KV_LORA, -1)], axis=1)

    def gains(gv):
        rope = gv[QK_NOPE:]
        return jnp.stack([gv[:QK_NOPE], _pad_lanes(rope), _pad_lanes(rope[_SWAP])])

    return {
        "w_a": w_a2.astype(BF16), "g_qa": g_qa[None], "w_uq": w_uq2.astype(BF16),
        "g_kva": g_kva[None], "w_ukv": w_ukv2.astype(BF16), "g_q": gains(g_q), "g_k": gains(g_k),
    }


def _rope_tables(n):
    pos = np.arange(n)
    n_freq = QK_ROPE // 4
    inv = ROPE_BASE ** (-np.arange(n_freq, dtype=np.float64) / n_freq)
    ang_r = (pos // GRID_W).astype(np.float64)[:, None] * inv
    ang_c = (pos % GRID_W).astype(np.float64)[:, None] * inv
    zeros = np.zeros((n, LANES - QK_ROPE), np.float64)
    cos = np.concatenate([np.cos(ang_r), np.cos(ang_r), np.cos(ang_c), np.cos(ang_c), zeros], axis=1)
    sin = np.concatenate([-np.sin(ang_r), np.sin(ang_r), -np.sin(ang_c), np.sin(ang_c), zeros], axis=1)
    return jnp.asarray(cos, F32), jnp.asarray(sin, F32)


def _identity_tables(n):
    ones = np.concatenate([np.ones((n, QK_ROPE), np.float32), np.zeros((n, LANES - QK_ROPE), np.float32)], axis=1)
    return jnp.asarray(ones, F32), jnp.zeros((n, LANES), F32)


def kernel(x, c, ctx, c_ctx, w_mod, b_mod, g_norm, ffn_w1, ffn_w3, ffn_w2, sc_w_in, sc_conv, sc_w_out,
           mla_w_a, mla_g_qa, mla_w_uq, mla_g_kva, mla_w_ukv, mla_g_q, mla_g_k, mla_w_o):
    bsz, n, d = x.shape
    n_ctx = ctx.shape[1]
    depth = w_mod.shape[0]

    cond = jnp.concatenate([c, c_ctx[None], jnp.zeros((MOD_ROWS - bsz - 1, d), F32)], axis=0)
    mod = _modulation(cond, w_mod, b_mod).reshape(depth, MOD_ROWS, 3, 3, d)

    g_all = g_norm.reshape(depth * 3, 1, d)
    ffn_f32 = (ffn_w1, ffn_w3, ffn_w2)
    cos_x, sin_x = _rope_tables(n)
    cos_c, sin_c = _identity_tables(n_ctx)

    def ffn_pair(h_lat, h_ctx, layer, sub, weights, outproj=None, mixer_casts=()):
        which = sub // 2
        nxt = (layer, 1) if which == 0 else (layer + 1, 0)
        casts = [(w, nxt) for w in ffn_f32] if nxt[0] < depth else []
        casts += list(mixer_casts)
        h_lat, cast = _ffn(h_lat, mod, layer, sub, True, g_all, weights, FFN_TILE, outproj, casts)
        if h_ctx is not None:
            flat, _ = _ffn(h_ctx.reshape(1, bsz * n_ctx, d), mod, layer, sub, False, g_all, weights, FFN_TILE)
            h_ctx = flat.reshape(bsz, n_ctx, d)
        n_next = len(casts) - len(mixer_casts)
        return h_lat, h_ctx, cast[:n_next], cast[n_next:]

    weights = tuple(w[0, 0].astype(BF16) for w in ffn_f32)
    h_x, h_c = x, ctx
    for i in range(depth):
        kind, j = i % 2, i // 2
        last = i == depth - 1
        run_ctx_in = (not last) or kind == 1
        run_ctx_out = not last

        mixer_f32 = [(sc_w_in, (j,)), (sc_w_out, (j,))] if kind == 0 else [(mla_w_o, (j,))]
        h_x, h_c_new, weights, mixer_w = ffn_pair(h_x, h_c if run_ctx_in else None, i, 0, weights,
                                                  mixer_casts=mixer_f32)
        if run_ctx_in:
            h_c = h_c_new

        pending_outproj = None
        if kind == 0:
            w_in, w_out = mixer_w
            h_x = _shortconv(h_x, mod, i, True, g_all, w_in, sc_conv, w_out, j)
            if run_ctx_out:
                h_c = _shortconv(h_c, mod, i, False, g_all, w_in, sc_conv, w_out, j)
        else:
            (w_o,) = mixer_w
            wts = _mla_weights(mla_w_a[j], mla_g_qa[j], mla_w_uq[j], mla_g_kva[j], mla_w_ukv[j],
                               mla_g_q[j], mla_g_k[j])
            q_c, k_c, v_c = _mla_proj(h_c, mod, i, False, g_all, wts, cos_c, sin_c, n_ctx)
            q_x, k_x, v_x = _mla_proj(h_x, mod, i, True, g_all, wts, cos_x, sin_x, PROJ_TILE)
            bound = _score_bound(mla_g_q[j], mla_g_k[j])
            pending_outproj = (_attention(bound, q_x, [k_c, k_x], [v_c, v_x]), w_o)
            if run_ctx_out:
                o_c = _attention(bound, q_c, [k_c], [v_c])
                h_c = _outproj(h_c, o_c, mod, i, False, w_o, n_ctx)

        h_x, h_c_new, weights, _ = ffn_pair(h_x, h_c if run_ctx_out else None, i, 2, weights, pending_outproj)
        if run_ctx_out:
            h_c = h_c_new
    return h_x
```

```python
import functools
import math

import jax
import jax.numpy as jnp
import numpy as np
from jax.experimental import pallas as pl
from jax.experimental.pallas import tpu as pltpu

GRID_W = 64
MLA_HEADS = 8
QK_NOPE = 128
QK_ROPE = 64
QK_HEAD = QK_NOPE + QK_ROPE
V_HEAD = 128
Q_LORA = 256
KV_LORA = 128
ROPE_BASE = 10000.0
QK_SCALE = QK_HEAD ** -0.5
Q_FOLD = QK_SCALE * math.log2(math.e)
EPS = 1e-6

LANES = 128
SUBLANES = 8
BF16_SUBLANES = 16
QK_PAD = 2 * LANES
VMEM_LIMIT = 56 * 1024 * 1024
MOD_ROWS = 24
MOD_COL_BLOCKS = 4

FFN_TILE = 1024
FFN_PIECE = 128
FFN_FUSED_PIECE = 512
SUB = 256
PROJ_TILE = 1024
ATTN_CHUNK = 1024
ATTN_STEP_QUERIES = 8192

MAX_BOUND_SHIFT = 48.0
BOUND_MARGIN = 1.02

BF16 = jnp.bfloat16
F32 = jnp.float32

_CONTRACT_LAST = (((1,), (1,)), ((), ()))
_CONTRACT_FIRST = (((0,), (0,)), ((), ()))


def _dot(a, b):
    return jnp.dot(a, b, preferred_element_type=F32)


def _dot_t(a_t, b):
    return jax.lax.dot_general(a_t, b, _CONTRACT_FIRST, preferred_element_type=F32)


def _resident(shape):
    nd = len(shape)
    return pl.BlockSpec(shape, lambda *_: (0,) * nd, pipeline_mode=pl.Buffered(1))


def _stacked(shape, lead):
    nl = len(lead)
    tail = len(shape) - nl
    return pl.BlockSpec((1,) * nl + tuple(shape[nl:]), lambda *_: tuple(lead) + (0,) * tail,
                        pipeline_mode=pl.Buffered(1))


def _params(n_grid):
    return pltpu.CompilerParams(
        dimension_semantics=("arbitrary",) * n_grid, vmem_limit_bytes=VMEM_LIMIT)


def _unit_rows(x):
    n = x.shape[-1]
    return x * jax.lax.rsqrt(jnp.sum(x * x, axis=-1, keepdims=True) + n * EPS)


def _norm_gain(gain):
    return math.sqrt(gain.shape[-1]) * gain


def _pre(h, gain_scale, shift):
    return _unit_rows(h) * gain_scale + shift


def _silu(a):
    return a * jax.nn.sigmoid(a)


def _mod_kernel(cond_ref, w_ref, b_ref, o_ref):
    cond = _silu(cond_ref[...]).astype(BF16)
    o_ref[0] = _dot(cond, w_ref[0].astype(BF16)) + b_ref[0]


def _modulation(cond, w_mod, b_mod):
    depth, d, nm = w_mod.shape
    tn = nm // MOD_COL_BLOCKS
    return pl.pallas_call(
        _mod_kernel,
        grid=(depth, nm // tn),
        in_specs=[
            pl.BlockSpec((MOD_ROWS, d), lambda i, j: (0, 0)),
            pl.BlockSpec((1, d, tn), lambda i, j: (i, 0, j)),
            pl.BlockSpec((1, 1, tn), lambda i, j: (i, 0, j)),
        ],
        out_specs=pl.BlockSpec((1, MOD_ROWS, tn), lambda i, j: (i, 0, j)),
        out_shape=jax.ShapeDtypeStruct((depth, MOD_ROWS, nm), F32),
        compiler_params=_params(2),
        name="adaln_mod",
    )(cond, w_mod, b_mod.reshape(depth, 1, nm))


def _mod_spec(layer, sub, per_sample, d):
    if per_sample:
        return pl.BlockSpec((1, 1, 1, 3, d), lambda b, t: (layer, b, sub, 0, 0))
    ctx_row = MOD_ROWS - SUBLANES
    return pl.BlockSpec((1, 1, 1, 3, d), lambda b, t: (layer, ctx_row, sub, 0, 0))


def _gain_spec(layer, sub, d):
    return pl.BlockSpec((1, 1, d), lambda b, t: (3 * layer + sub, 0, 0))


def _ffn_kernel(*refs, fused_outproj, n_casts):
    refs = list(refs)
    h_ref = refs.pop(0)
    if fused_outproj:
        ot_ref, mix_ref, wo_ref = refs.pop(0), refs.pop(0), refs.pop(0)
    mod_ref, g_ref, w1_ref, w3_ref, w2_ref = (refs.pop(0) for _ in range(5))
    cast_srcs = [refs.pop(0) for _ in range(n_casts)]
    o_ref = refs.pop(0)
    for src, dst in zip(cast_srcs, refs):
        dst[...] = src[(0,) * (len(src.shape) - 2)].astype(BF16)
    mod = mod_ref[0, 0, 0]
    gain_scale = _norm_gain(g_ref[0]) * (1.0 + mod[1:2])
    shift = mod[0:1]
    half_gate = 0.5 * mod[2:3]
    tm = h_ref.shape[1]
    sub = min(FFN_FUSED_PIECE if fused_outproj else FFN_PIECE, tm)
    for s in range(tm // sub):
        rows = slice(s * sub, (s + 1) * sub)
        h = h_ref[0, rows, :]
        if fused_outproj:
            h = h + mix_ref[0, 0, 0][2:3] * _dot_t(ot_ref[0, :, rows], wo_ref[...])
        xn = _pre(h, gain_scale, shift).astype(BF16)
        a = _dot(xn, w1_ref[...])
        b = _dot(xn, w3_ref[...])
        gated = (_silu(a) * b).astype(BF16)
        o_ref[0, rows, :] = h + half_gate * _dot(gated, w2_ref[...])


def _row_chunks(n_rows, n_steps):
    for steps_per_chunk in (1, 2, 4, 8):
        rows, rem = divmod(n_rows * steps_per_chunk, n_steps)
        if rem == 0 and rows % BF16_SUBLANES == 0:
            return rows, steps_per_chunk
    raise ValueError(f"cannot split {n_rows} weight rows over {n_steps} grid steps")


def _ffn(h, mod, layer, sub, per_sample, g_all, weights, tm, outproj=None, casts=()):
    bsz, n, d = h.shape
    nt = n // tm
    tile = pl.BlockSpec((1, tm, d), lambda b, t: (b, t, 0))
    specs = [_mod_spec(layer, sub, per_sample, d), _gain_spec(layer, sub, d)] + [_resident(w.shape) for w in weights]
    args = [mod, g_all, *weights]
    if outproj is not None:
        o_t, w_o = outproj
        specs = [pl.BlockSpec((1, o_t.shape[1], tm), lambda b, t: (b, 0, t)),
                 _mod_spec(layer, 1, per_sample, d), _resident(w_o.shape)] + specs
        args = [o_t, mod, w_o] + args
    out_specs = [tile]
    out_shapes = [jax.ShapeDtypeStruct(h.shape, F32)]
    for w, lead in casts:
        n_rows, cols = w.shape[-2:]
        rows, spc = _row_chunks(n_rows, bsz * nt)
        specs.append(pl.BlockSpec((1,) * len(lead) + (rows, cols),
                                  lambda b, t, spc=spc, lead=tuple(lead): lead + ((b * nt + t) // spc, 0)))
        args.append(w)
        out_specs.append(pl.BlockSpec((rows, cols), lambda b, t, spc=spc: ((b * nt + t) // spc, 0)))
        out_shapes.append(jax.ShapeDtypeStruct((n_rows, cols), BF16))
    outs = pl.pallas_call(
        functools.partial(_ffn_kernel, fused_outproj=outproj is not None, n_casts=len(casts)),
        grid=(bsz, nt),
        in_specs=[tile] + specs,
        out_specs=out_specs,
        out_shape=out_shapes,
        compiler_params=_params(2),
        name="ffn",
    )(h, *args)
    return outs[0], tuple(outs[1:])


def _shortconv_kernel(h_ref, mod_ref, g_ref, win_ref, cw_ref, wout_ref, o_ref, cu_scr):
    n, d = h_ref.shape[1], h_ref.shape[2]
    sub = min(SUB, n)
    n_sub = n // sub
    mod = mod_ref[0, 0, 0]
    gain_scale = _norm_gain(g_ref[0]) * (1.0 + mod[1:2])
    shift = mod[0:1]
    gate = mod[2:3]
    cw = cw_ref[0]
    zero_row = jnp.zeros((1, d), F32)
    cu_scr[SUBLANES - 1:SUBLANES, :] = zero_row
    cu_scr[n + SUBLANES:n + SUBLANES + 1, :] = zero_row

    def project(s):
        rows = slice(s * sub, (s + 1) * sub)
        xn = _pre(h_ref[0, rows, :], gain_scale, shift).astype(BF16)
        p = _dot(xn, win_ref[...])
        cu_scr[SUBLANES + s * sub:SUBLANES + (s + 1) * sub, :] = p[:, d:2 * d] * p[:, 2 * d:]
        return p[:, :d]

    def mix(s, b_gate):
        rows = slice(s * sub, (s + 1) * sub)
        lo = SUBLANES + s * sub
        conv = (cw[0:1] * cu_scr[lo - 1:lo - 1 + sub, :] + cw[1:2] * cu_scr[lo:lo + sub, :]
                + cw[2:3] * cu_scr[lo + 1:lo + 1 + sub, :])
        y = (b_gate * conv).astype(BF16)
        o_ref[0, rows, :] = h_ref[0, rows, :] + gate * _dot(y, wout_ref[...])

    b_gate = project(0)
    for s in range(n_sub):
        b_next = project(s + 1) if s + 1 < n_sub else None
        mix(s, b_gate)
        b_gate = b_next


def _shortconv(h, mod, layer, per_sample, g_all, w_in, conv_all, w_out, j):
    bsz, n, d = h.shape
    tile = pl.BlockSpec((1, n, d), lambda b, t: (b, 0, 0))
    return pl.pallas_call(
        _shortconv_kernel,
        grid=(bsz, 1),
        in_specs=[tile, _mod_spec(layer, 1, per_sample, d), _gain_spec(layer, 1, d),
                  _resident(w_in.shape), _stacked(conv_all.shape, (j,)), _resident(w_out.shape)],
        out_specs=tile,
        out_shape=jax.ShapeDtypeStruct(h.shape, F32),
        scratch_shapes=[pltpu.VMEM((n + 2 * SUBLANES, d), F32)],
        compiler_params=_params(2),
        name="shortconv",
    )(h, mod, g_all, w_in, conv_all, w_out)


def _mla_proj_kernel(h_ref, mod_ref, g_ref, wa_ref, gqa_ref, wuq_ref, gkva_ref, wukv_ref,
                     gq_ref, gk_ref, cos_ref, sin_ref, q_ref, k_ref, v_ref):
    h = h_ref[0]
    mod = mod_ref[0, 0, 0]
    xn = _pre(h, _norm_gain(g_ref[0]) * (1.0 + mod[1:2]), mod[0:1]).astype(BF16)
    a = _dot(xn, wa_ref[...])
    cq = a[:, :Q_LORA]
    ckv = a[:, Q_LORA:Q_LORA + KV_LORA]
    kr = a[:, Q_LORA + KV_LORA:Q_LORA + KV_LORA + LANES]
    krs = a[:, Q_LORA + KV_LORA + LANES:]

    cqn = (_unit_rows(cq) * _norm_gain(gqa_ref[...])).astype(BF16)
    ckvn = (_unit_rows(ckv) * _norm_gain(gkva_ref[...])).astype(BF16)
    qraw = _dot(cqn, wuq_ref[...])
    kvraw = _dot(ckvn, wukv_ref[...])

    cos = cos_ref[...]
    sin = sin_ref[...]
    gq = gq_ref[...]
    gk = gk_ref[...] * (QK_HEAD * Q_FOLD)
    hw = MLA_HEADS * LANES
    nope_gain = gq[0:1] * gk[0:1]
    kr_rot = kr * (cos * gk[1:2]) + krs * (sin * gk[2:3])
    kr_sq = kr * kr
    q_cos = cos * gq[1:2]
    q_sin = sin * gq[2:3]
    norm_eps = QK_HEAD * EPS
    for hd in range(MLA_HEADS):
        lo = hd * LANES
        qn = qraw[:, lo:lo + LANES]
        qr = qraw[:, hw + lo:hw + lo + LANES]
        qrs = qraw[:, 2 * hw + lo:2 * hw + lo + LANES]
        r = jax.lax.rsqrt(jnp.sum(qn * qn + qr * qr, axis=-1, keepdims=True) + norm_eps)
        q_ref[0, hd, :, :LANES] = (qn * r).astype(BF16)
        q_ref[0, hd, :, LANES:] = ((qr * q_cos + qrs * q_sin) * r).astype(BF16)

        kn = kvraw[:, lo:lo + LANES]
        rk = jax.lax.rsqrt(jnp.sum(kn * kn + kr_sq, axis=-1, keepdims=True) + norm_eps)
        k_ref[0, hd, :, :LANES] = (kn * nope_gain * rk).astype(BF16)
        k_ref[0, hd, :, LANES:] = (kr_rot * rk).astype(BF16)
        v_ref[0, hd] = kvraw[:, hw + lo:hw + lo + LANES].astype(BF16)


def _mla_proj(h, mod, layer, per_sample, g_all, wts, cos, sin, tm):
    bsz, n, d = h.shape
    tile = pl.BlockSpec((1, tm, d), lambda b, t: (b, t, 0))
    tab = pl.BlockSpec((tm, LANES), lambda b, t: (t, 0))
    qk_spec = pl.BlockSpec((1, MLA_HEADS, tm, QK_PAD), lambda b, t: (b, 0, t, 0))
    v_spec = pl.BlockSpec((1, MLA_HEADS, tm, V_HEAD), lambda b, t: (b, 0, t, 0))
    qk_shape = jax.ShapeDtypeStruct((bsz, MLA_HEADS, n, QK_PAD), BF16)
    v_shape = jax.ShapeDtypeStruct((bsz, MLA_HEADS, n, V_HEAD), BF16)
    small = [wts["w_a"], wts["g_qa"], wts["w_uq"], wts["g_kva"], wts["w_ukv"], wts["g_q"], wts["g_k"]]
    return pl.pallas_call(
        _mla_proj_kernel,
        grid=(bsz, n // tm),
        in_specs=[tile, _mod_spec(layer, 1, per_sample, d), _gain_spec(layer, 1, d)]
                 + [_resident(w.shape) for w in small] + [tab, tab],
        out_specs=[qk_spec, qk_spec, v_spec],
        out_shape=[qk_shape, qk_shape, v_shape],
        compiler_params=_params(2),
        name="mla_proj",
    )(h, mod, g_all, *small, cos, sin)


def _attn_kernel(*refs, n_parts):
    bound_ref, q_ref = refs[0], refs[1]
    k_refs = refs[2:2 + n_parts]
    v_refs = refs[2 + n_parts:2 + 2 * n_parts]
    o_ref = refs[2 + 2 * n_parts]
    s_scr = refs[3 + 2 * n_parts]
    n_slots, _, chunk = s_scr.shape
    items = [(hd, c) for hd in range(q_ref.shape[1]) for c in range(q_ref.shape[2] // chunk)]
    sizes = [k.shape[2] for k in k_refs]
    offs = [sum(sizes[:i]) for i in range(n_parts)]

    def q_chunk(hd, c):
        return q_ref[0, hd, c * chunk:(c + 1) * chunk, :]

    def store(hd, c, out, denom):
        o_ref[0, hd * V_HEAD:(hd + 1) * V_HEAD, c * chunk:(c + 1) * chunk] = (
            out * (1.0 / denom)).astype(o_ref.dtype)

    bound = bound_ref[0]
    shift_by_bound = bound <= MAX_BOUND_SHIFT

    @pl.when(shift_by_bound)
    def _():
        for hd, c in items:
            q = q_chunk(hd, c)
            denom = None
            out = None
            for k_ref, v_ref in zip(k_refs, v_refs):
                s = jax.lax.dot_general(k_ref[0, hd], q, _CONTRACT_LAST, preferred_element_type=F32)
                p = jnp.exp2(s - bound)
                part = jnp.sum(p, axis=0, keepdims=True)
                denom = part if denom is None else denom + part
                acc = _dot_t(v_ref[0, hd], p.astype(BF16))
                out = acc if out is None else out + acc
            store(hd, c, out, denom)

    @pl.when(jnp.logical_not(shift_by_bound))
    def _():
        def scores(i):
            hd, c = items[i]
            q = q_chunk(hd, c)
            col_max = None
            for k_ref, off, sz in zip(k_refs, offs, sizes):
                s = jax.lax.dot_general(k_ref[0, hd], q, _CONTRACT_LAST, preferred_element_type=F32)
                s_scr[i % n_slots, off:off + sz, :] = s
                part = jnp.max(s, axis=0, keepdims=True)
                col_max = part if col_max is None else jnp.maximum(col_max, part)
            return col_max

        def attend(i, col_max):
            hd, c = items[i]
            denom = None
            out = None
            for v_ref, off, sz in zip(v_refs, offs, sizes):
                p = jnp.exp2(s_scr[i % n_slots, off:off + sz, :] - col_max)
                part = jnp.sum(p, axis=0, keepdims=True)
                denom = part if denom is None else denom + part
                acc = _dot_t(v_ref[0, hd], p.astype(BF16))
                out = acc if out is None else out + acc
            store(hd, c, out, denom)

        col_max = scores(0)
        for i in range(len(items)):
            next_max = scores(i + 1) if i + 1 < len(items) else None
            attend(i, col_max)
            col_max = next_max


def _score_bound(g_q, g_k):
    q_gain = jnp.maximum(1.0, jnp.max(jnp.abs(g_q[QK_NOPE:])))
    k_gain = jnp.maximum(jnp.max(jnp.abs(g_q[:QK_NOPE] * g_k[:QK_NOPE])), jnp.max(jnp.abs(g_k[QK_NOPE:])))
    return (QK_HEAD * Q_FOLD * BOUND_MARGIN * q_gain * k_gain).reshape(1)


def _attention(bound, q, ks, vs):
    bsz, nh, n, _ = q.shape
    n_parts = len(ks)
    chunk = min(ATTN_CHUNK, n)
    n_keys = sum(k.shape[2] for k in ks)
    heads = max(1, min(nh, ATTN_STEP_QUERIES // n))
    n_items = heads * (n // chunk)
    q_spec = pl.BlockSpec((1, heads, n, QK_PAD), lambda b, h, *_: (b, h, 0, 0))
    k_specs = [pl.BlockSpec((1, heads, k.shape[2], QK_PAD), lambda b, h, *_: (b, h, 0, 0)) for k in ks]
    v_specs = [pl.BlockSpec((1, heads, v.shape[2], V_HEAD), lambda b, h, *_: (b, h, 0, 0)) for v in vs]
    return pl.pallas_call(
        functools.partial(_attn_kernel, n_parts=n_parts),
        grid_spec=pltpu.PrefetchScalarGridSpec(
            num_scalar_prefetch=1,
            grid=(bsz, nh // heads),
            in_specs=[q_spec] + k_specs + v_specs,
            out_specs=pl.BlockSpec((1, heads * V_HEAD, n), lambda b, h, *_: (b, h, 0)),
            scratch_shapes=[pltpu.VMEM((min(2, n_items), n_keys, chunk), F32)],
        ),
        out_shape=jax.ShapeDtypeStruct((bsz, nh * V_HEAD, n), BF16),
        compiler_params=_params(2),
        name="attention",
    )(bound, q, *ks, *vs)


def _outproj_kernel(h_ref, o_ref, mod_ref, wo_ref, out_ref):
    mod = mod_ref[0, 0, 0]
    out_ref[0] = h_ref[0] + mod[2:3] * _dot_t(o_ref[0], wo_ref[...])


def _outproj(h, o, mod, layer, per_sample, w_o, tm):
    bsz, n, d = h.shape
    tile = pl.BlockSpec((1, tm, d), lambda b, t: (b, t, 0))
    o_tile = pl.BlockSpec((1, o.shape[1], tm), lambda b, t: (b, 0, t))
    return pl.pallas_call(
        _outproj_kernel,
        grid=(bsz, n // tm),
        in_specs=[tile, o_tile, _mod_spec(layer, 1, per_sample, d), _resident(w_o.shape)],
        out_specs=tile,
        out_shape=jax.ShapeDtypeStruct(h.shape, F32),
        compiler_params=_params(2),
        name="attn_outproj",
    )(h, o, mod, w_o)


_SWAP = np.concatenate([np.arange(16, 32), np.arange(0, 16), np.arange(48, 64), np.arange(32, 48)])


def _pad_lanes(w):
    return jnp.concatenate([w, jnp.zeros_like(w)], axis=-1)


def _mla_weights(w_a, g_qa, w_uq, g_kva, w_ukv, g_q, g_k):
    kr = w_a[:, Q_LORA + KV_LORA:]
    w_a2 = jnp.concatenate(
        [w_a[:, :Q_LORA + KV_LORA], _pad_lanes(kr), _pad_lanes(kr[:, _SWAP])], axis=1)
    uq = w_uq.reshape(Q_LORA, MLA_HEADS, QK_HEAD)
    uq_rope = uq[:, :, QK_NOPE:]
    w_uq2 = jnp.concatenate(
        [uq[:, :, :QK_NOPE].reshape(Q_LORA, -1),
         _pad_lanes(uq_rope).reshape(Q_LORA, -1),
         _pad_lanes(uq_rope[:, :, _SWAP]).reshape(Q_LORA, -1)], axis=1)
    ukv = w_ukv.reshape(KV_LORA, MLA_HEADS, QK_NOPE + V_HEAD)
    w_ukv2 = jnp.concatenate(
        [ukv[:, :, :QK_NOPE].reshape(KV_LORA, -1), ukv[:, :, QK_NOPE:].reshape(KV_LORA, -1)], axis=1)

    def gains(gv):
        rope = gv[QK_NOPE:]
        return jnp.stack([gv[:QK_NOPE], _pad_lanes(rope), _pad_lanes(rope[_SWAP])])

    return {
        "w_a": w_a2.astype(BF16), "g_qa": g_qa[None], "w_uq": w_uq2.astype(BF16),
        "g_kva": g_kva[None], "w_ukv": w_ukv2.astype(BF16), "g_q": gains(g_q), "g_k": gains(g_k),
    }


def _rope_tables(n):
    pos = np.arange(n)
    n_freq = QK_ROPE // 4
    inv = ROPE_BASE ** (-np.arange(n_freq, dtype=np.float64) / n_freq)
    ang_r = (pos // GRID_W).astype(np.float64)[:, None] * inv
    ang_c = (pos % GRID_W).astype(np.float64)[:, None] * inv
    zeros = np.zeros((n, LANES - QK_ROPE), np.float64)
    cos = np.concatenate([np.cos(ang_r), np.cos(ang_r), np.cos(ang_c), np.cos(ang_c), zeros], axis=1)
    sin = np.concatenate([-np.sin(ang_r), np.sin(ang_r), -np.sin(ang_c), np.sin(ang_c), zeros], axis=1)
    return jnp.asarray(cos, F32), jnp.asarray(sin, F32)


def _identity_tables(n):
    ones = np.concatenate([np.ones((n, QK_ROPE), np.float32), np.zeros((n, LANES - QK_ROPE), np.float32)], axis=1)
    return jnp.asarray(ones, F32), jnp.zeros((n, LANES), F32)


def kernel(x, c, ctx, c_ctx, w_mod, b_mod, g_norm, ffn_w1, ffn_w3, ffn_w2, sc_w_in, sc_conv, sc_w_out,
           mla_w_a, mla_g_qa, mla_w_uq, mla_g_kva, mla_w_ukv, mla_g_q, mla_g_k, mla_w_o):
    bsz, n, d = x.shape
    n_ctx = ctx.shape[1]
    depth = w_mod.shape[0]

    cond = jnp.concatenate([c, c_ctx[None], jnp.zeros((MOD_ROWS - bsz - 1, d), F32)], axis=0)
    mod = _modulation(cond, w_mod, b_mod).reshape(depth, MOD_ROWS, 3, 3, d)

    g_all = g_norm.reshape(depth * 3, 1, d)
    ffn_f32 = (ffn_w1, ffn_w3, ffn_w2)
    cos_x, sin_x = _rope_tables(n)
    cos_c, sin_c = _identity_tables(n_ctx)

    def ffn_pair(h_lat, h_ctx, layer, sub, weights, outproj=None, mixer_casts=()):
        which = sub // 2
        nxt = (layer, 1) if which == 0 else (layer + 1, 0)
        casts = [(w, nxt) for w in ffn_f32] if nxt[0] < depth else []
        casts += list(mixer_casts)
        h_lat, cast = _ffn(h_lat, mod, layer, sub, True, g_all, weights, FFN_TILE, outproj, casts)
        if h_ctx is not None:
            flat, _ = _ffn(h_ctx.reshape(1, bsz * n_ctx, d), mod, layer, sub, False, g_all, weights, FFN_TILE)
            h_ctx = flat.reshape(bsz, n_ctx, d)
        n_next = len(casts) - len(mixer_casts)
        return h_lat, h_ctx, cast[:n_next], cast[n_next:]

    weights = tuple(w[0, 0].astype(BF16) for w in ffn_f32)
    h_x, h_c = x, ctx
    for i in range(depth):
        kind, j = i % 2, i // 2
        last = i == depth - 1
        run_ctx_in = (not last) or kind == 1
        run_ctx_out = not last

        mixer_f32 = [(sc_w_in, (j,)), (sc_w_out, (j,))] if kind == 0 else [(mla_w_o, (j,))]
        h_x, h_c_new, weights, mixer_w = ffn_pair(h_x, h_c if run_ctx_in else None, i, 0, weights,
                                                  mixer_casts=mixer_f32)
        if run_ctx_in:
            h_c = h_c_new

        pending_outproj = None
        if kind == 0:
            w_in, w_out = mixer_w
            h_x = _shortconv(h_x, mod, i, True, g_all, w_in, sc_conv, w_out, j)
            if run_ctx_out:
                h_c = _shortconv(h_c, mod, i, False, g_all, w_in, sc_conv, w_out, j)
        else:
            (w_o,) = mixer_w
            wts = _mla_weights(mla_w_a[j], mla_g_qa[j], mla_w_uq[j], mla_g_kva[j], mla_w_ukv[j],
                               mla_g_q[j], mla_g_k[j])
            q_c, k_c, v_c = _mla_proj(h_c, mod, i, False, g_all, wts, cos_c, sin_c, n_ctx)
            q_x, k_x, v_x = _mla_proj(h_x, mod, i, True, g_all, wts, cos_x, sin_x, PROJ_TILE)
            bound = _score_bound(mla_g_q[j], mla_g_k[j])
            pending_outproj = (_attention(bound, q_x, [k_c, k_x], [v_c, v_x]), w_o)
            if run_ctx_out:
                o_c = _attention(bound, q_c, [k_c], [v_c])
                h_c = _outproj(h_c, o_c, mod, i, False, w_o, n_ctx)

        h_x, h_c_new, weights, _ = ffn_pair(h_x, h_c if run_ctx_out else None, i, 2, weights, pending_outproj)
        if run_ctx_out:
            h_c = h_c_new
    return h_x
```

```python
import functools
import math

import jax
import jax.numpy as jnp
import numpy as np
from jax.experimental import pallas as pl
from jax.experimental.pallas import tpu as pltpu

GRID_W = 64
MLA_HEADS = 8
QK_NOPE = 128
QK_ROPE = 64
QK_HEAD = QK_NOPE + QK_ROPE
V_HEAD = 128
Q_LORA = 256
KV_LORA = 128
ROPE_BASE = 10000.0
QK_SCALE = QK_HEAD ** -0.5
Q_FOLD = QK_SCALE * math.log2(math.e)
EPS = 1e-6

LANES = 128
SUBLANES = 8
BF16_SUBLANES = 16
QK_PAD = 2 * LANES
VMEM_LIMIT = 56 * 1024 * 1024
MOD_ROWS = 24
MOD_COL_BLOCKS = 4

FFN_TILE = 1024
FFN_PIECE = 256
FFN_FUSED_PIECE = 512
SUB = 256
PROJ_TILE = 1024
ATTN_CHUNK = 1024
ATTN_STEP_QUERIES = 8192

MAX_BOUND_SHIFT = 48.0
BOUND_MARGIN = 1.02

BF16 = jnp.bfloat16
F32 = jnp.float32

_CONTRACT_LAST = (((1,), (1,)), ((), ()))
_CONTRACT_FIRST = (((0,), (0,)), ((), ()))


def _dot(a, b):
    return jnp.dot(a, b, preferred_element_type=F32)


def _dot_t(a_t, b):
    return jax.lax.dot_general(a_t, b, _CONTRACT_FIRST, preferred_element_type=F32)


def _resident(shape):
    nd = len(shape)
    return pl.BlockSpec(shape, lambda *_: (0,) * nd, pipeline_mode=pl.Buffered(1))


def _stacked(shape, lead):
    nl = len(lead)
    tail = len(shape) - nl
    return pl.BlockSpec((1,) * nl + tuple(shape[nl:]), lambda *_: tuple(lead) + (0,) * tail,
                        pipeline_mode=pl.Buffered(1))


def _params(n_grid):
    return pltpu.CompilerParams(
        dimension_semantics=("arbitrary",) * n_grid, vmem_limit_bytes=VMEM_LIMIT)


def _unit_rows(x):
    n = x.shape[-1]
    return x * jax.lax.rsqrt(jnp.sum(x * x, axis=-1, keepdims=True) + n * EPS)


def _norm_gain(gain):
    return math.sqrt(gain.shape[-1]) * gain


def _pre(h, gain_scale, shift):
    return _unit_rows(h) * gain_scale + shift


def _silu(a):
    return a * jax.nn.sigmoid(a)


def _mod_kernel(cond_ref, w_ref, b_ref, o_ref):
    cond = _silu(cond_ref[...]).astype(BF16)
    o_ref[0] = _dot(cond, w_ref[0].astype(BF16)) + b_ref[0]


def _modulation(cond, w_mod, b_mod):
    depth, d, nm = w_mod.shape
    tn = nm // MOD_COL_BLOCKS
    return pl.pallas_call(
        _mod_kernel,
        grid=(depth, nm // tn),
        in_specs=[
            pl.BlockSpec((MOD_ROWS, d), lambda i, j: (0, 0)),
            pl.BlockSpec((1, d, tn), lambda i, j: (i, 0, j)),
            pl.BlockSpec((1, 1, tn), lambda i, j: (i, 0, j)),
        ],
        out_specs=pl.BlockSpec((1, MOD_ROWS, tn), lambda i, j: (i, 0, j)),
        out_shape=jax.ShapeDtypeStruct((depth, MOD_ROWS, nm), F32),
        compiler_params=_params(2),
        name="adaln_mod",
    )(cond, w_mod, b_mod.reshape(depth, 1, nm))


def _mod_spec(layer, sub, per_sample, d):
    if per_sample:
        return pl.BlockSpec((1, 1, 1, 3, d), lambda b, t: (layer, b, sub, 0, 0))
    ctx_row = MOD_ROWS - SUBLANES
    return pl.BlockSpec((1, 1, 1, 3, d), lambda b, t: (layer, ctx_row, sub, 0, 0))


def _gain_spec(layer, sub, d):
    return pl.BlockSpec((1, 1, d), lambda b, t: (3 * layer + sub, 0, 0))


def _ffn_kernel(*refs, fused_outproj, n_casts):
    refs = list(refs)
    h_ref = refs.pop(0)
    if fused_outproj:
        ot_ref, mix_ref, wo_ref = refs.pop(0), refs.pop(0), refs.pop(0)
    mod_ref, g_ref, w1_ref, w3_ref, w2_ref = (refs.pop(0) for _ in range(5))
    cast_srcs = [refs.pop(0) for _ in range(n_casts)]
    o_ref = refs.pop(0)
    for src, dst in zip(cast_srcs, refs):
        dst[...] = src[(0,) * (len(src.shape) - 2)].astype(BF16)
    mod = mod_ref[0, 0, 0]
    gain_scale = _norm_gain(g_ref[0]) * (1.0 + mod[1:2])
    shift = mod[0:1]
    half_gate = 0.5 * mod[2:3]
    tm = h_ref.shape[1]
    sub = min(FFN_FUSED_PIECE if fused_outproj else FFN_PIECE, tm)
    for s in range(tm // sub):
        rows = slice(s * sub, (s + 1) * sub)
        h = h_ref[0, rows, :]
        if fused_outproj:
            h = h + mix_ref[0, 0, 0][2:3] * _dot_t(ot_ref[0, :, rows], wo_ref[...])
        xn = _pre(h, gain_scale, shift).astype(BF16)
        a = _dot(xn, w1_ref[...])
        b = _dot(xn, w3_ref[...])
        gated = (_silu(a) * b).astype(BF16)
        o_ref[0, rows, :] = h + half_gate * _dot(gated, w2_ref[...])


def _row_chunks(n_rows, n_steps):
    for steps_per_chunk in (1, 2, 4, 8):
        rows, rem = divmod(n_rows * steps_per_chunk, n_steps)
        if rem == 0 and rows % BF16_SUBLANES == 0:
            return rows, steps_per_chunk
    raise ValueError(f"cannot split {n_rows} weight rows over {n_steps} grid steps")


def _ffn(h, mod, layer, sub, per_sample, g_all, weights, tm, outproj=None, casts=()):
    bsz, n, d = h.shape
    nt = n // tm
    tile = pl.BlockSpec((1, tm, d), lambda b, t: (b, t, 0))
    specs = [_mod_spec(layer, sub, per_sample, d), _gain_spec(layer, sub, d)] + [_resident(w.shape) for w in weights]
    args = [mod, g_all, *weights]
    if outproj is not None:
        o_t, w_o = outproj
        specs = [pl.BlockSpec((1, o_t.shape[1], tm), lambda b, t: (b, 0, t)),
                 _mod_spec(layer, 1, per_sample, d), _resident(w_o.shape)] + specs
        args = [o_t, mod, w_o] + args
    out_specs = [tile]
    out_shapes = [jax.ShapeDtypeStruct(h.shape, F32)]
    for w, lead in casts:
        n_rows, cols = w.shape[-2:]
        rows, spc = _row_chunks(n_rows, bsz * nt)
        specs.append(pl.BlockSpec((1,) * len(lead) + (rows, cols),
                                  lambda b, t, spc=spc, lead=tuple(lead): lead + ((b * nt + t) // spc, 0)))
        args.append(w)
        out_specs.append(pl.BlockSpec((rows, cols), lambda b, t, spc=spc: ((b * nt + t) // spc, 0)))
        out_shapes.append(jax.ShapeDtypeStruct((n_rows, cols), BF16))
    outs = pl.pallas_call(
        functools.partial(_ffn_kernel, fused_outproj=outproj is not None, n_casts=len(casts)),
        grid=(bsz, nt),
        in_specs=[tile] + specs,
        out_specs=out_specs,
        out_shape=out_shapes,
        compiler_params=_params(2),
        name="ffn",
    )(h, *args)
    return outs[0], tuple(outs[1:])


def _shortconv_kernel(h_ref, mod_ref, g_ref, win_ref, cw_ref, wout_ref, o_ref, cu_scr):
    n, d = h_ref.shape[1], h_ref.shape[2]
    sub = min(SUB, n)
    n_sub = n // sub
    mod = mod_ref[0, 0, 0]
    gain_scale = _norm_gain(g_ref[0]) * (1.0 + mod[1:2])
    shift = mod[0:1]
    gate = mod[2:3]
    cw = cw_ref[0]
    zero_row = jnp.zeros((1, d), F32)
    cu_scr[SUBLANES - 1:SUBLANES, :] = zero_row
    cu_scr[n + SUBLANES:n + SUBLANES + 1, :] = zero_row

    def project(s):
        rows = slice(s * sub, (s + 1) * sub)
        xn = _pre(h_ref[0, rows, :], gain_scale, shift).astype(BF16)
        p = _dot(xn, win_ref[...])
        cu_scr[SUBLANES + s * sub:SUBLANES + (s + 1) * sub, :] = p[:, d:2 * d] * p[:, 2 * d:]
        return p[:, :d]

    def mix(s, b_gate):
        rows = slice(s * sub, (s + 1) * sub)
        lo = SUBLANES + s * sub
        conv = (cw[0:1] * cu_scr[lo - 1:lo - 1 + sub, :] + cw[1:2] * cu_scr[lo:lo + sub, :]
                + cw[2:3] * cu_scr[lo + 1:lo + 1 + sub, :])
        y = (b_gate * conv).astype(BF16)
        o_ref[0, rows, :] = h_ref[0, rows, :] + gate * _dot(y, wout_ref[...])

    b_gate = project(0)
    for s in range(n_sub):
        b_next = project(s + 1) if s + 1 < n_sub else None
        mix(s, b_gate)
        b_gate = b_next


def _shortconv(h, mod, layer, per_sample, g_all, w_in, conv_all, w_out, j):
    bsz, n, d = h.shape
    tile = pl.BlockSpec((1, n, d), lambda b, t: (b, 0, 0))
    return pl.pallas_call(
        _shortconv_kernel,
        grid=(bsz, 1),
        in_specs=[tile, _mod_spec(layer, 1, per_sample, d), _gain_spec(layer, 1, d),
                  _resident(w_in.shape), _stacked(conv_all.shape, (j,)), _resident(w_out.shape)],
        out_specs=tile,
        out_shape=jax.ShapeDtypeStruct(h.shape, F32),
        scratch_shapes=[pltpu.VMEM((n + 2 * SUBLANES, d), F32)],
        compiler_params=_params(2),
        name="shortconv",
    )(h, mod, g_all, w_in, conv_all, w_out)


def _mla_proj_kernel(h_ref, mod_ref, g_ref, wa_ref, gqa_ref, wuq_ref, gkva_ref, wukv_ref,
                     gq_ref, gk_ref, cos_ref, sin_ref, q_ref, k_ref, v_ref):
    h = h_ref[0]
    mod = mod_ref[0, 0, 0]
    xn = _pre(h, _norm_gain(g_ref[0]) * (1.0 + mod[1:2]), mod[0:1]).astype(BF16)
    a = _dot(xn, wa_ref[...])
    cq = a[:, :Q_LORA]
    ckv = a[:, Q_LORA:Q_LORA + KV_LORA]
    kr = a[:, Q_LORA + KV_LORA:Q_LORA + KV_LORA + LANES]
    krs = a[:, Q_LORA + KV_LORA + LANES:]

    cqn = (_unit_rows(cq) * _norm_gain(gqa_ref[...])).astype(BF16)
    ckvn = (_unit_rows(ckv) * _norm_gain(gkva_ref[...])).astype(BF16)
    qraw = _dot(cqn, wuq_ref[...])
    kvraw = _dot(ckvn, wukv_ref[...])

    cos = cos_ref[...]
    sin = sin_ref[...]
    gq = gq_ref[...]
    gk = gk_ref[...] * (QK_HEAD * Q_FOLD)
    hw = MLA_HEADS * LANES
    nope_gain = gq[0:1] * gk[0:1]
    kr_rot = kr * (cos * gk[1:2]) + krs * (sin * gk[2:3])
    kr_sq = kr * kr
    q_cos = cos * gq[1:2]
    q_sin = sin * gq[2:3]
    norm_eps = QK_HEAD * EPS
    for hd in range(MLA_HEADS):
        lo = hd * LANES
        qn = qraw[:, lo:lo + LANES]
        qr = qraw[:, hw + lo:hw + lo + LANES]
        qrs = qraw[:, 2 * hw + lo:2 * hw + lo + LANES]
        r = jax.lax.rsqrt(jnp.sum(qn * qn + qr * qr, axis=-1, keepdims=True) + norm_eps)
        q_ref[0, hd, :, :LANES] = (qn * r).astype(BF16)
        q_ref[0, hd, :, LANES:] = ((qr * q_cos + qrs * q_sin) * r).astype(BF16)

        kn = kvraw[:, lo:lo + LANES]
        rk = jax.lax.rsqrt(jnp.sum(kn * kn + kr_sq, axis=-1, keepdims=True) + norm_eps)
        k_ref[0, hd, :, :LANES] = (kn * nope_gain * rk).astype(BF16)
        k_ref[0, hd, :, LANES:] = (kr_rot * rk).astype(BF16)
        v_ref[0, hd] = kvraw[:, hw + lo:hw + lo + LANES].astype(BF16)


def _mla_proj(h, mod, layer, per_sample, g_all, wts, cos, sin, tm):
    bsz, n, d = h.shape
    tile = pl.BlockSpec((1, tm, d), lambda b, t: (b, t, 0))
    tab = pl.BlockSpec((tm, LANES), lambda b, t: (t, 0))
    qk_spec = pl.BlockSpec((1, MLA_HEADS, tm, QK_PAD), lambda b, t: (b, 0, t, 0))
    v_spec = pl.BlockSpec((1, MLA_HEADS, tm, V_HEAD), lambda b, t: (b, 0, t, 0))
    qk_shape = jax.ShapeDtypeStruct((bsz, MLA_HEADS, n, QK_PAD), BF16)
    v_shape = jax.ShapeDtypeStruct((bsz, MLA_HEADS, n, V_HEAD), BF16)
    small = [wts["w_a"], wts["g_qa"], wts["w_uq"], wts["g_kva"], wts["w_ukv"], wts["g_q"], wts["g_k"]]
    return pl.pallas_call(
        _mla_proj_kernel,
        grid=(bsz, n // tm),
        in_specs=[tile, _mod_spec(layer, 1, per_sample, d), _gain_spec(layer, 1, d)]
                 + [_resident(w.shape) for w in small] + [tab, tab],
        out_specs=[qk_spec, qk_spec, v_spec],
        out_shape=[qk_shape, qk_shape, v_shape],
        compiler_params=_params(2),
        name="mla_proj",
    )(h, mod, g_all, *small, cos, sin)


def _attn_kernel(*refs, n_parts):
    bound_ref, q_ref = refs[0], refs[1]
    k_refs = refs[2:2 + n_parts]
    v_refs = refs[2 + n_parts:2 + 2 * n_parts]
    o_ref = refs[2 + 2 * n_parts]
    s_scr = refs[3 + 2 * n_parts]
    n_slots, _, chunk = s_scr.shape
    items = [(hd, c) for hd in range(q_ref.shape[1]) for c in range(q_ref.shape[2] // chunk)]
    sizes = [k.shape[2] for k in k_refs]
    offs = [sum(sizes[:i]) for i in range(n_parts)]

    def q_chunk(hd, c):
        return q_ref[0, hd, c * chunk:(c + 1) * chunk, :]

    def store(hd, c, out, denom):
        o_ref[0, hd * V_HEAD:(hd + 1) * V_HEAD, c * chunk:(c + 1) * chunk] = (
            out * (1.0 / denom)).astype(o_ref.dtype)

    bound = bound_ref[0]
    shift_by_bound = bound <= MAX_BOUND_SHIFT

    @pl.when(shift_by_bound)
    def _():
        for hd, c in items:
            q = q_chunk(hd, c)
            denom = None
            out = None
            for k_ref, v_ref in zip(k_refs, v_refs):
                s = jax.lax.dot_general(k_ref[0, hd], q, _CONTRACT_LAST, preferred_element_type=F32)
                p = jnp.exp2(s - bound)
                part = jnp.sum(p, axis=0, keepdims=True)
                denom = part if denom is None else denom + part
                acc = _dot_t(v_ref[0, hd], p.astype(BF16))
                out = acc if out is None else out + acc
            store(hd, c, out, denom)

    @pl.when(jnp.logical_not(shift_by_bound))
    def _():
        def scores(i):
            hd, c = items[i]
            q = q_chunk(hd, c)
            col_max = None
            for k_ref, off, sz in zip(k_refs, offs, sizes):
                s = jax.lax.dot_general(k_ref[0, hd], q, _CONTRACT_LAST, preferred_element_type=F32)
                s_scr[i % n_slots, off:off + sz, :] = s
                part = jnp.max(s, axis=0, keepdims=True)
                col_max = part if col_max is None else jnp.maximum(col_max, part)
            return col_max

        def attend(i, col_max):
            hd, c = items[i]
            denom = None
            out = None
            for v_ref, off, sz in zip(v_refs, offs, sizes):
                p = jnp.exp2(s_scr[i % n_slots, off:off + sz, :] - col_max)
                part = jnp.sum(p, axis=0, keepdims=True)
                denom = part if denom is None else denom + part
                acc = _dot_t(v_ref[0, hd], p.astype(BF16))
                out = acc if out is None else out + acc
            store(hd, c, out, denom)

        col_max = scores(0)
        for i in range(len(items)):
            next_max = scores(i + 1) if i + 1 < len(items) else None
            attend(i, col_max)
            col_max = next_max


def _score_bound(g_q, g_k):
    q_gain = jnp.maximum(1.0, jnp.max(jnp.abs(g_q[QK_NOPE:])))
    k_gain = jnp.maximum(jnp.max(jnp.abs(g_q[:QK_NOPE] * g_k[:QK_NOPE])), jnp.max(jnp.abs(g_k[QK_NOPE:])))
    return (QK_HEAD * Q_FOLD * BOUND_MARGIN * q_gain * k_gain).reshape(1)


def _attention(bound, q, ks, vs):
    bsz, nh, n, _ = q.shape
    n_parts = len(ks)
    chunk = min(ATTN_CHUNK, n)
    n_keys = sum(k.shape[2] for k in ks)
    heads = max(1, min(nh, ATTN_STEP_QUERIES // n))
    n_items = heads * (n // chunk)
    q_spec = pl.BlockSpec((1, heads, n, QK_PAD), lambda b, h, *_: (b, h, 0, 0))
    k_specs = [pl.BlockSpec((1, heads, k.shape[2], QK_PAD), lambda b, h, *_: (b, h, 0, 0)) for k in ks]
    v_specs = [pl.BlockSpec((1, heads, v.shape[2], V_HEAD), lambda b, h, *_: (b, h, 0, 0)) for v in vs]
    return pl.pallas_call(
        functools.partial(_attn_kernel, n_parts=n_parts),
        grid_spec=pltpu.PrefetchScalarGridSpec(
            num_scalar_prefetch=1,
            grid=(bsz, nh // heads),
            in_specs=[q_spec] + k_specs + v_specs,
            out_specs=pl.BlockSpec((1, heads * V_HEAD, n), lambda b, h, *_: (b, h, 0)),
            scratch_shapes=[pltpu.VMEM((min(2, n_items), n_keys, chunk), F32)],
        ),
        out_shape=jax.ShapeDtypeStruct((bsz, nh * V_HEAD, n), BF16),
        compiler_params=_params(2),
        name="attention",
    )(bound, q, *ks, *vs)


def _outproj_kernel(h_ref, o_ref, mod_ref, wo_ref, out_ref):
    mod = mod_ref[0, 0, 0]
    out_ref[0] = h_ref[0] + mod[2:3] * _dot_t(o_ref[0], wo_ref[...])


def _outproj(h, o, mod, layer, per_sample, w_o, tm):
    bsz, n, d = h.shape
    tile = pl.BlockSpec((1, tm, d), lambda b, t: (b, t, 0))
    o_tile = pl.BlockSpec((1, o.shape[1], tm), lambda b, t: (b, 0, t))
    return pl.pallas_call(
        _outproj_kernel,
        grid=(bsz, n // tm),
        in_specs=[tile, o_tile, _mod_spec(layer, 1, per_sample, d), _resident(w_o.shape)],
        out_specs=tile,
        out_shape=jax.ShapeDtypeStruct(h.shape, F32),
        compiler_params=_params(2),
        name="attn_outproj",
    )(h, o, mod, w_o)


_SWAP = np.concatenate([np.arange(16, 32), np.arange(0, 16), np.arange(48, 64), np.arange(32, 48)])


def _pad_lanes(w):
    return jnp.concatenate([w, jnp.zeros_like(w)], axis=-1)


def _mla_weights(w_a, g_qa, w_uq, g_kva, w_ukv, g_q, g_k):
    kr = w_a[:, Q_LORA + KV_LORA:]
    w_a2 = jnp.concatenate(
        [w_a[:, :Q_LORA + KV_LORA], _pad_lanes(kr), _pad_lanes(kr[:, _SWAP])], axis=1)
    uq = w_uq.reshape(Q_LORA, MLA_HEADS, QK_HEAD)
    uq_rope = uq[:, :, QK_NOPE:]
    w_uq2 = jnp.concatenate(
        [uq[:, :, :QK_NOPE].reshape(Q_LORA, -1),
         _pad_lanes(uq_rope).reshape(Q_LORA, -1),
         _pad_lanes(uq_rope[:, :, _SWAP]).reshape(Q_LORA, -1)], axis=1)
    ukv = w_ukv.reshape(KV_LORA, MLA_HEADS, QK_NOPE + V_HEAD)
    w_ukv2 = jnp.concatenate(
        [ukv[:, :, :QK_NOPE].reshape(KV_LORA, -1), ukv[:, :, QK_NOPE:].reshape(KV_LORA, -1)], axis=1)

    def gains(gv):
        rope = gv[QK_NOPE:]
        return jnp.stack([gv[:QK_NOPE], _pad_lanes(rope), _pad_lanes(rope[_SWAP])])

    return {
        "w_a": w_a2.astype(BF16), "g_qa": g_qa[None], "w_uq": w_uq2.astype(BF16),
        "g_kva": g_kva[None], "w_ukv": w_ukv2.astype(BF16), "g_q": gains(g_q), "g_k": gains(g_k),
    }


def _rope_tables(n):
    pos = np.arange(n)
    n_freq = QK_ROPE // 4
    inv = ROPE_BASE ** (-np.arange(n_freq, dtype=np.float64) / n_freq)
    ang_r = (pos // GRID_W).astype(np.float64)[:, None] * inv
    ang_c = (pos % GRID_W).astype(np.float64)[:, None] * inv
    zeros = np.zeros((n, LANES - QK_ROPE), np.float64)
    cos = np.concatenate([np.cos(ang_r), np.cos(ang_r), np.cos(ang_c), np.cos(ang_c), zeros], axis=1)
    sin = np.concatenate([-np.sin(ang_r), np.sin(ang_r), -np.sin(ang_c), np.sin(ang_c), zeros], axis=1)
    return jnp.asarray(cos, F32), jnp.asarray(sin, F32)


def _identity_tables(n):
    ones = np.concatenate([np.ones((n, QK_ROPE), np.float32), np.zeros((n, LANES - QK_ROPE), np.float32)], axis=1)
    return jnp.asarray(ones, F32), jnp.zeros((n, LANES), F32)


def kernel(x, c, ctx, c_ctx, w_mod, b_mod, g_norm, ffn_w1, ffn_w3, ffn_w2, sc_w_in, sc_conv, sc_w_out,
           mla_w_a, mla_g_qa, mla_w_uq, mla_g_kva, mla_w_ukv, mla_g_q, mla_g_k, mla_w_o):
    bsz, n, d = x.shape
    n_ctx = ctx.shape[1]
    depth = w_mod.shape[0]

    cond = jnp.concatenate([c, c_ctx[None], jnp.zeros((MOD_ROWS - bsz - 1, d), F32)], axis=0)
    mod = _modulation(cond, w_mod, b_mod).reshape(depth, MOD_ROWS, 3, 3, d)

    g_all = g_norm.reshape(depth * 3, 1, d)
    ffn_f32 = (ffn_w1, ffn_w3, ffn_w2)
    cos_x, sin_x = _rope_tables(n)
    cos_c, sin_c = _identity_tables(n_ctx)

    def ffn_pair(h_lat, h_ctx, layer, sub, weights, outproj=None, mixer_casts=()):
        which = sub // 2
        nxt = (layer, 1) if which == 0 else (layer + 1, 0)
        casts = [(w, nxt) for w in ffn_f32] if nxt[0] < depth else []
        casts += list(mixer_casts)
        h_lat, cast = _ffn(h_lat, mod, layer, sub, True, g_all, weights, FFN_TILE, outproj, casts)
        if h_ctx is not None:
            flat, _ = _ffn(h_ctx.reshape(1, bsz * n_ctx, d), mod, layer, sub, False, g_all, weights, FFN_TILE)
            h_ctx = flat.reshape(bsz, n_ctx, d)
        n_next = len(casts) - len(mixer_casts)
        return h_lat, h_ctx, cast[:n_next], cast[n_next:]

    weights = tuple(w[0, 0].astype(BF16) for w in ffn_f32)
    h_x, h_c = x, ctx
    for i in range(depth):
        kind, j = i % 2, i // 2
        last = i == depth - 1
        run_ctx_in = (not last) or kind == 1
        run_ctx_out = not last

        mixer_f32 = [(sc_w_in, (j,)), (sc_w_out, (j,))] if kind == 0 else [(mla_w_o, (j,))]
        h_x, h_c_new, weights, mixer_w = ffn_pair(h_x, h_c if run_ctx_in else None, i, 0, weights,
                                                  mixer_casts=mixer_f32)
        if run_ctx_in:
            h_c = h_c_new

        pending_outproj = None
        if kind == 0:
            w_in, w_out = mixer_w
            h_x = _shortconv(h_x, mod, i, True, g_all, w_in, sc_conv, w_out, j)
            if run_ctx_out:
                h_c = _shortconv(h_c, mod, i, False, g_all, w_in, sc_conv, w_out, j)
        else:
            (w_o,) = mixer_w
            wts = _mla_weights(mla_w_a[j], mla_g_qa[j], mla_w_uq[j], mla_g_kva[j], mla_w_ukv[j],
                               mla_g_q[j], mla_g_k[j])
            q_c, k_c, v_c = _mla_proj(h_c, mod, i, False, g_all, wts, cos_c, sin_c, n_ctx)
            q_x, k_x, v_x = _mla_proj(h_x, mod, i, True, g_all, wts, cos_x, sin_x, PROJ_TILE)
            bound = _score_bound(mla_g_q[j], mla_g_k[j])
            pending_outproj = (_attention(bound, q_x, [k_c, k_x], [v_c, v_x]), w_o)
            if run_ctx_out:
                o_c = _attention(bound, q_c, [k_c], [v_c])
                h_c = _outproj(h_c, o_c, mod, i, False, w_o, n_ctx)

        h_x, h_c_new, weights, _ = ffn_pair(h_x, h_c if run_ctx_out else None, i, 2, weights, pending_outproj)
        if run_ctx_out:
            h_c = h_c_new
    return h_x
```
